```python
import math
import jax, jax.numpy as jnp
from jax import lax
import numpy as np

D_MODEL = 2048
BATCH = 32
SEQ = 256
DEPTH = 4
DEC_BATCH = 2
DEC_SEQ = 1024
PAST_LEN = 512

GRID_W = 64
N_MIXERS = 2
N_SSD_LAYERS = (DEPTH + N_MIXERS - 1) // N_MIXERS
N_ATTN_LAYERS = DEPTH // N_MIXERS
N_MOD = 9
D_FF = 5632
SSD_EXPAND = 2
D_INNER = SSD_EXPAND * D_MODEL
SSD_HEADDIM = 64
SSD_HEADS = D_INNER // SSD_HEADDIM
SSD_GROUPS = 8
SSD_STATE = 128
SSD_CHUNK = 128
D_CONV = 5
CONV_DIM = D_INNER + 2 * SSD_GROUPS * SSD_STATE
D_IN_PROJ = D_INNER + CONV_DIM + 2 * SSD_HEADS
HEAD_DIM = 128
ATTN_HEADS = D_MODEL // HEAD_DIM
ATTN_KV_HEADS = 4
GQA_GROUP = ATTN_HEADS // ATTN_KV_HEADS
QKV_DIM = (ATTN_HEADS + 2 * ATTN_KV_HEADS) * HEAD_DIM
ROPE_PAIRS_PER_AXIS = HEAD_DIM // 4
ROPE_THETA = 10000.0
Q_BLOCK = 128
EPS = 1e-6

kernel_name = 'hybrid_ssd_axial_gqa_diffusion_step'


def rmsnorm(x, g):
    xf = x.astype(jnp.float32)
    y = xf * lax.rsqrt(jnp.mean(xf * xf, axis=-1, keepdims=True) + EPS)
    return (y * g.astype(jnp.float32)).astype(x.dtype)


def modulation(cond, w, b):
    m = jax.nn.silu(cond) @ w + b
    return m.reshape(cond.shape[0], N_MOD, D_MODEL)


def pre_norm(x, mod, j, g):
    return rmsnorm(x, g) * (1 + mod[:, None, 3 * j + 1]) + mod[:, None, 3 * j]


def post_norm_residual(x, out, mod, j, g, res_w):
    return x + res_w * mod[:, None, 3 * j + 2] * rmsnorm(out, g)


def swiglu(h, w1, w3, w2):
    return (jax.nn.silu(h @ w1) * (h @ w3)) @ w2


def ffn_sublayer(x, mod, j, g_pre, g_post, w1, w3, w2):
    out = swiglu(pre_norm(x, mod, j, g_pre), w1, w3, w2)
    return post_norm_residual(x, out, mod, j, g_post, 0.5)


def centred_dwconv(x, w, b):
    y = lax.conv_general_dilated(
        x, w[:, None, :], window_strides=(1,),
        padding=[(D_CONV // 2, D_CONV // 2)],
        dimension_numbers=('NWC', 'WIO', 'NWC'),
        feature_group_count=x.shape[-1])
    return y + b


def ssd_scan(x, dt, A, B, C, s0):
    b, l, h, p = x.shape
    g, n = B.shape[-2:]
    e = h // g
    nc = l // SSD_CHUNK
    f32 = jnp.float32
    xdt = (x.astype(f32) * dt[..., None]).reshape(b, nc, SSD_CHUNK, g, e, p)
    a = (dt * A.astype(f32)).reshape(b, nc, SSD_CHUNK, g, e)
    Bc = B.astype(f32).reshape(b, nc, SSD_CHUNK, g, n)
    Cc = C.astype(f32).reshape(b, nc, SSD_CHUNK, g, n)
    a_cum = jnp.cumsum(a, axis=2)
    seg = a_cum[:, :, :, None] - a_cum[:, :, None, :]
    mask = jnp.tril(jnp.ones((SSD_CHUNK, SSD_CHUNK), dtype=bool))[:, :, None, None]
    L = jnp.exp(jnp.where(mask, seg, -jnp.inf))
    CB = jnp.einsum('bcqgn,bcsgn->bcqsg', Cc, Bc)
    y_diag = jnp.einsum('bcqsg,bcqsge,bcsgep->bcqgep', CB, L, xdt)
    decay_states = jnp.exp(a_cum[:, :, -1:] - a_cum)
    states = jnp.einsum('bcsgn,bcsge,bcsgep->bcgepn', Bc, decay_states, xdt)
    chunk_decay = jnp.exp(a_cum[:, :, -1])

    def step(s, inp):
        dec, st = inp
        return s * dec[..., None, None] + st, s

    s_init = s0.astype(f32).reshape(b, g, e, p, n)
    s_final, s_prev = lax.scan(step, s_init, (jnp.moveaxis(chunk_decay, 1, 0), jnp.moveaxis(states, 1, 0)))
    s_prev = jnp.moveaxis(s_prev, 0, 1)
    y_off = jnp.einsum('bcqgn,bcgepn,bcqge->bcqgep', Cc, s_prev, jnp.exp(a_cum))
    y = (y_diag + y_off).reshape(b, l, h, p)
    return y, s_final.reshape(b, h, p, n)


def ssd_mixer(h, s0_fwd, s0_bwd, in_w, conv_w, conv_b, dt_bias, a_log, d_skip, norm_w, out_w):
    b, l, _ = h.shape
    f32 = jnp.float32
    zxbcdt = h @ in_w
    z = zxbcdt[..., :D_INNER]
    xbc = zxbcdt[..., D_INNER:D_INNER + CONV_DIM]
    dt_raw = zxbcdt[..., D_INNER + CONV_DIM:].astype(f32)
    xbc = jax.nn.silu(centred_dwconv(xbc, conv_w, conv_b))
    gn = SSD_GROUPS * SSD_STATE
    xs = xbc[..., :D_INNER].reshape(b, l, SSD_HEADS, SSD_HEADDIM)
    Bm = xbc[..., D_INNER:D_INNER + gn].reshape(b, l, SSD_GROUPS, SSD_STATE)
    Cm = xbc[..., D_INNER + gn:].reshape(b, l, SSD_GROUPS, SSD_STATE)
    dt_f = jax.nn.softplus(dt_raw[..., :SSD_HEADS] + dt_bias[0].astype(f32))
    dt_b = jax.nn.softplus(dt_raw[..., SSD_HEADS:] + dt_bias[1].astype(f32))
    A_f = -jnp.exp(a_log[0].astype(f32))
    A_b = -jnp.exp(a_log[1].astype(f32))
    y_f, s_f = ssd_scan(xs, dt_f, A_f, Bm, Cm, s0_fwd)
    y_b, s_b = ssd_scan(jnp.flip(xs, 1), jnp.flip(dt_b, 1), A_b, jnp.flip(Bm, 1), jnp.flip(Cm, 1), s0_bwd)
    y = y_f + jnp.flip(y_b, 1) + d_skip.astype(f32)[:, None] * xs.astype(f32)
    y = y.reshape(b, l, D_INNER).astype(h.dtype)
    y = rmsnorm(y * jax.nn.silu(z), norm_w)
    return y @ out_w, s_f, s_b


def gqa_project(h, qkv_w, q_norm, k_norm):
    b, l, _ = h.shape
    qkv = h @ qkv_w
    nq = ATTN_HEADS * HEAD_DIM
    nk = ATTN_KV_HEADS * HEAD_DIM
    q = qkv[..., :nq].reshape(b, l, ATTN_HEADS, HEAD_DIM)
    k = qkv[..., nq:nq + nk].reshape(b, l, ATTN_KV_HEADS, HEAD_DIM)
    v = qkv[..., nq + nk:].reshape(b, l, ATTN_KV_HEADS, HEAD_DIM)
    return rmsnorm(q, q_norm), rmsnorm(k, k_norm), v


def axial_rope(n_tok):
    rows = n_tok // GRID_W
    row_id = jnp.broadcast_to(jnp.arange(rows, dtype=jnp.float32)[:, None], (rows, GRID_W)).reshape(-1)
    col_id = jnp.broadcast_to(jnp.arange(GRID_W, dtype=jnp.float32)[None, :], (rows, GRID_W)).reshape(-1)
    inv = ROPE_THETA ** (-jnp.arange(ROPE_PAIRS_PER_AXIS, dtype=jnp.float32) / ROPE_PAIRS_PER_AXIS)
    ang = jnp.concatenate([row_id[:, None] * inv, col_id[:, None] * inv], axis=-1)
    return jnp.cos(ang), jnp.sin(ang)


def apply_rope(x, cos, sin):
    x2 = x.astype(jnp.float32).reshape(*x.shape[:-1], HEAD_DIM // 2, 2)
    x0, x1 = x2[..., 0], x2[..., 1]
    c = cos[None, :, None, :]
    s = sin[None, :, None, :]
    out = jnp.stack([x0 * c - x1 * s, x0 * s + x1 * c], axis=-1)
    return out.reshape(x.shape).astype(x.dtype)


def blocked_attention(q, k, v):
    b, lq = q.shape[:2]
    nb = lq // Q_BLOCK
    qb = jnp.moveaxis(q.reshape(b, nb, Q_BLOCK, ATTN_KV_HEADS, GQA_GROUP, HEAD_DIM), 1, 0)
    scale = HEAD_DIM ** -0.5

    def one_block(qblk):
        s = jnp.einsum('bqkgd,bskd->bkgqs', qblk, k, preferred_element_type=jnp.float32) * scale
        p = jax.nn.softmax(s, axis=-1)
        return jnp.einsum('bkgqs,bskd->bqkgd', p.astype(v.dtype), v)

    o = lax.map(one_block, qb)
    return jnp.moveaxis(o, 0, 1).reshape(b, lq, ATTN_HEADS * HEAD_DIM)


def setup_inputs(seed: int = 0) -> dict:
    key = jax.random.key(seed)
    ks = jax.random.split(key, 26)
    f32 = jnp.float32

    def nrm(k, shape, scale):
        return jax.random.normal(k, shape, f32) * scale

    x_prompt = nrm(ks[0], (BATCH, SEQ, D_MODEL), 1.0)
    x_sample = nrm(ks[1], (DEC_BATCH, DEC_SEQ, D_MODEL), 1.0)
    c = nrm(ks[2], (DEC_BATCH, D_MODEL), 1.0)
    state_ssd = nrm(ks[3], (DEC_BATCH, N_SSD_LAYERS, 2, SSD_HEADS, SSD_HEADDIM, SSD_STATE), 0.5)
    cache_k = nrm(ks[4], (DEC_BATCH, N_ATTN_LAYERS, PAST_LEN, ATTN_KV_HEADS, HEAD_DIM), 1.0)
    cache_v = nrm(ks[5], (DEC_BATCH, N_ATTN_LAYERS, PAST_LEN, ATTN_KV_HEADS, HEAD_DIM), 1.0)
    c_ctx = nrm(ks[6], (D_MODEL,), 1.0)
    ada_w = nrm(ks[7], (DEPTH, D_MODEL, N_MOD * D_MODEL), 0.5 * D_MODEL ** -0.5)
    ada_b = nrm(ks[8], (DEPTH, N_MOD * D_MODEL), 0.02)
    norm_pre = 1.0 + nrm(ks[9], (DEPTH, 3, D_MODEL), 0.02)
    norm_post = 1.0 + nrm(ks[10], (DEPTH, 3, D_MODEL), 0.02)
    ffn_w1 = nrm(ks[11], (DEPTH, 2, D_MODEL, D_FF), D_MODEL ** -0.5)
    ffn_w3 = nrm(ks[12], (DEPTH, 2, D_MODEL, D_FF), D_MODEL ** -0.5)
    ffn_w2 = nrm(ks[13], (DEPTH, 2, D_FF, D_MODEL), D_FF ** -0.5)
    ssd_in_w = nrm(ks[14], (N_SSD_LAYERS, D_MODEL, D_IN_PROJ), D_MODEL ** -0.5)
    ssd_conv_w = nrm(ks[15], (N_SSD_LAYERS, D_CONV, CONV_DIM), D_CONV ** -0.5)
    ssd_conv_b = nrm(ks[16], (N_SSD_LAYERS, CONV_DIM), 0.02)
    dt0 = jnp.exp(jax.random.uniform(ks[17], (N_SSD_LAYERS, 2, SSD_HEADS), f32, math.log(1e-3), math.log(1e-1)))
    ssd_dt_bias = dt0 + jnp.log(-jnp.expm1(-dt0))
    ssd_a_log = jnp.log(jax.random.uniform(ks[18], (N_SSD_LAYERS, 2, SSD_HEADS), f32, 1.0, 16.0))
    ssd_d = 1.0 + nrm(ks[19], (N_SSD_LAYERS, SSD_HEADS), 0.02)
    ssd_norm_w = 1.0 + nrm(ks[20], (N_SSD_LAYERS, D_INNER), 0.02)
    ssd_out_w = nrm(ks[21], (N_SSD_LAYERS, D_INNER, D_MODEL), D_INNER ** -0.5)
    attn_qkv_w = nrm(ks[22], (N_ATTN_LAYERS, D_MODEL, QKV_DIM), D_MODEL ** -0.5)
    attn_q_norm = 1.0 + nrm(ks[23], (N_ATTN_LAYERS, HEAD_DIM), 0.02)
    attn_k_norm = 1.0 + nrm(ks[24], (N_ATTN_LAYERS, HEAD_DIM), 0.02)
    attn_o_w = nrm(ks[25], (N_ATTN_LAYERS, ATTN_HEADS * HEAD_DIM, D_MODEL), (ATTN_HEADS * HEAD_DIM) ** -0.5)
    return {'x_prompt': x_prompt, 'x_sample': x_sample, 'c': c, 'state_ssd': state_ssd,
            'cache_k': cache_k, 'cache_v': cache_v, 'c_ctx': c_ctx, 'ada_w': ada_w, 'ada_b': ada_b,
            'norm_pre': norm_pre, 'norm_post': norm_post, 'ffn_w1': ffn_w1, 'ffn_w3': ffn_w3,
            'ffn_w2': ffn_w2, 'ssd_in_w': ssd_in_w, 'ssd_conv_w': ssd_conv_w, 'ssd_conv_b': ssd_conv_b,
            'ssd_dt_bias': ssd_dt_bias, 'ssd_a_log': ssd_a_log, 'ssd_d': ssd_d, 'ssd_norm_w': ssd_norm_w,
            'ssd_out_w': ssd_out_w, 'attn_qkv_w': attn_qkv_w, 'attn_q_norm': attn_q_norm,
            'attn_k_norm': attn_k_norm, 'attn_o_w': attn_o_w}


def reference(x_prompt, x_sample, c, state_ssd, cache_k, cache_v, c_ctx, ada_w, ada_b, norm_pre,
              norm_post, ffn_w1, ffn_w3, ffn_w2, ssd_in_w, ssd_conv_w, ssd_conv_b, ssd_dt_bias,
              ssd_a_log, ssd_d, ssd_norm_w, ssd_out_w, attn_qkv_w, attn_q_norm, attn_k_norm, attn_o_w):
    h = x_prompt
    b_ctx = h.shape[0]
    ssd_states, ctx_keys, ctx_vals = [], [], []
    for l in range(DEPTH):
        mod = modulation(c_ctx[None, :], ada_w[l], ada_b[l])
        h = ffn_sublayer(h, mod, 0, norm_pre[l, 0], norm_post[l, 0], ffn_w1[l, 0], ffn_w3[l, 0], ffn_w2[l, 0])
        hm = pre_norm(h, mod, 1, norm_pre[l, 1])
        i = l // N_MIXERS
        if l % N_MIXERS == 0:
            zero = jnp.zeros((b_ctx, SSD_HEADS, SSD_HEADDIM, SSD_STATE), jnp.float32)
            out, s_f, s_b = ssd_mixer(hm, zero, zero, ssd_in_w[i], ssd_conv_w[i], ssd_conv_b[i],
                                      ssd_dt_bias[i], ssd_a_log[i], ssd_d[i], ssd_norm_w[i], ssd_out_w[i])
            ssd_states.append(jnp.stack([s_f, s_b], axis=1))
        else:
            q, k, v = gqa_project(hm, attn_qkv_w[i], attn_q_norm[i], attn_k_norm[i])
            out = blocked_attention(q, k, v) @ attn_o_w[i]
            ctx_keys.append(k)
            ctx_vals.append(v)
        h = post_norm_residual(h, out, mod, 1, norm_post[l, 1], 1.0)
        h = ffn_sublayer(h, mod, 2, norm_pre[l, 2], norm_post[l, 2], ffn_w1[l, 1], ffn_w3[l, 1], ffn_w2[l, 1])
    y_prompt = h
    new_state_ssd = jnp.stack(ssd_states, axis=1).astype(x_prompt.dtype)
    new_cache_k = jnp.stack(ctx_keys, axis=1)
    new_cache_v = jnp.stack(ctx_vals, axis=1)

    n_lat = x_sample.shape[1]
    cos, sin = axial_rope(n_lat)
    h = x_sample
    for l in range(DEPTH):
        mod = modulation(c, ada_w[l], ada_b[l])
        h = ffn_sublayer(h, mod, 0, norm_pre[l, 0], norm_post[l, 0], ffn_w1[l, 0], ffn_w3[l, 0], ffn_w2[l, 0])
        hm = pre_norm(h, mod, 1, norm_pre[l, 1])
        i = l // N_MIXERS
        if l % N_MIXERS == 0:
            out, _, _ = ssd_mixer(hm, state_ssd[:, i, 0], state_ssd[:, i, 1], ssd_in_w[i], ssd_conv_w[i],
                                  ssd_conv_b[i], ssd_dt_bias[i], ssd_a_log[i], ssd_d[i], ssd_norm_w[i],
                                  ssd_out_w[i])
        else:
            q, k, v = gqa_project(hm, attn_qkv_w[i], attn_q_norm[i], attn_k_norm[i])
            q = apply_rope(q, cos, sin)
            k = apply_rope(k, cos, sin)
            k_all = jnp.concatenate([cache_k[:, i].astype(k.dtype), k], axis=1)
            v_all = jnp.concatenate([cache_v[:, i].astype(v.dtype), v], axis=1)
            out = blocked_attention(q, k_all, v_all) @ attn_o_w[i]
        h = post_norm_residual(h, out, mod, 1, norm_post[l, 1], 1.0)
        h = ffn_sublayer(h, mod, 2, norm_pre[l, 2], norm_post[l, 2], ffn_w1[l, 1], ffn_w3[l, 1], ffn_w2[l, 1])
    y_sample = h
    return (y_prompt, y_sample, new_state_ssd, new_cache_k, new_cache_v)
```

```python
import functools

import jax
import jax.numpy as jnp
from jax import lax
from jax.experimental import pallas as pl
from jax.experimental.pallas import tpu as pltpu

F32 = jnp.float32
BF16 = jnp.bfloat16

N_MOD = 9
N_COND_ROWS = 8
SSD_HEADDIM = 64
SSD_GROUPS = 8
SSD_STATE = 128
SSD_CHUNK = 128
D_CONV = 5
HEAD_DIM = 128
GQA_GROUP = 4
GRID_W = 64
ROPE_THETA = 10000.0
EPS = 1e-6

V7X_VMEM_LIMIT_BYTES = 56 * 1024 * 1024
ROW_TILE = 1024
NORM_ROW_TILE = 256
W_CAST_ROWS = 256


def _params(sem):
    return pltpu.CompilerParams(dimension_semantics=sem, vmem_limit_bytes=V7X_VMEM_LIMIT_BYTES)


def _silu(x):
    return x * jax.nn.sigmoid(x)


def _cast_weight(w_ref, wb_ref):
    k = w_ref.shape[0]
    rows = W_CAST_ROWS if k % W_CAST_ROWS == 0 else k

    def body(r, carry):
        sl = pl.ds(pl.multiple_of(r * rows, rows), rows)
        wb_ref[sl, :] = w_ref[sl, :].astype(BF16)
        return carry

    lax.fori_loop(0, k // rows, body, 0)


def _mod_body(c_ref, w_ref, b_ref, o_ref):
    s = _silu(c_ref[...]).astype(BF16)
    o_ref[...] = jnp.dot(s, w_ref[...].astype(BF16), preferred_element_type=F32) + b_ref[...]


def _modulation(cond, ada_w, ada_b):
    depth, d, n = ada_w.shape
    tn = 1024
    return pl.pallas_call(
        _mod_body,
        grid=(depth, n // tn),
        in_specs=[
            pl.BlockSpec((N_COND_ROWS, d), lambda l, j: (0, 0)),
            pl.BlockSpec((None, d, tn), lambda l, j: (l, 0, j)),
            pl.BlockSpec((None, 1, tn), lambda l, j: (l, 0, j)),
        ],
        out_specs=pl.BlockSpec((None, N_COND_ROWS, tn), lambda l, j: (l, 0, j)),
        out_shape=jax.ShapeDtypeStruct((depth, N_COND_ROWS, n), F32),
        compiler_params=_params(("arbitrary", "arbitrary")),
        name="adaln_modulation",
    )(cond, ada_w, ada_b.reshape(depth, 1, n))


def _postpre_body(*refs, has_post, has_pre, res_w):
    it = iter(refs)
    x_ref = next(it)
    if has_post:
        y_ref, gpost_ref, gate_ref = next(it), next(it), next(it)
    if has_pre:
        gpre_ref, scale_ref, shift_ref = next(it), next(it), next(it)
    if has_post:
        xo_ref = next(it)
    if has_pre:
        h_ref = next(it)
    x = x_ref[...]
    if has_post:
        y = y_ref[...]
        r = lax.rsqrt(jnp.mean(y * y, axis=-1, keepdims=True) + EPS)
        x = x + (res_w * gate_ref[...]) * ((y * r) * gpost_ref[...])
        xo_ref[...] = x
    if has_pre:
        r = lax.rsqrt(jnp.mean(x * x, axis=-1, keepdims=True) + EPS)
        h = ((x * r) * gpre_ref[...]) * (1.0 + scale_ref[...]) + shift_ref[...]
        h_ref[...] = h.astype(BF16)


def _postpre(x, y, modr, n_ctx_rows, rows_per_req, post, pre):
    t, d = x.shape
    tm = NORM_ROW_TILE

    def crow(i):
        r0 = i * tm
        return jnp.where(r0 < n_ctx_rows, 0, 1 + (r0 - n_ctx_rows) // rows_per_req)

    def mod_spec(base, k):
        return pl.BlockSpec((None, 1, d), lambda i: (base + crow(i) * N_MOD + k, 0, 0))

    row_spec = pl.BlockSpec((tm, d), lambda i: (i, 0))
    vec_spec = pl.BlockSpec((1, d), lambda i: (0, 0))
    args, in_specs, out_shape, out_specs = [x], [row_spec], [], []
    res_w = 0.0
    if post is not None:
        g_post, base, res_w = post
        args += [y, g_post, modr]
        in_specs += [row_spec, vec_spec, mod_spec(base, 2)]
        out_shape.append(jax.ShapeDtypeStruct((t, d), F32))
        out_specs.append(row_spec)
    if pre is not None:
        g_pre, base = pre
        args += [g_pre, modr, modr]
        in_specs += [vec_spec, mod_spec(base, 1), mod_spec(base, 0)]
        out_shape.append(jax.ShapeDtypeStruct((t, d), BF16))
        out_specs.append(row_spec)
    outs = pl.pallas_call(
        functools.partial(_postpre_body, has_post=post is not None, has_pre=pre is not None, res_w=res_w),
        grid=(t // tm,),
        in_specs=in_specs,
        out_specs=out_specs,
        out_shape=out_shape,
        compiler_params=_params(("arbitrary",)),
        name="norm_modulate_residual",
    )(*args)
    return outs


def _mm_body(*refs, has_scale, inv_width):
    if has_scale:
        x_ref, w_ref, ss_ref, o_ref, wb_ref = refs
    else:
        x_ref, w_ref, o_ref, wb_ref = refs

    @pl.when(pl.program_id(1) == 0)
    def _():
        _cast_weight(w_ref, wb_ref)

    acc = jnp.dot(x_ref[...], wb_ref[...], preferred_element_type=F32)
    if has_scale:
        ms = jnp.sum(ss_ref[...], axis=-1, keepdims=True) * inv_width
        acc = acc * lax.rsqrt(ms + EPS)
    o_ref[...] = acc.astype(o_ref.dtype)


def _mm(x, w, *, tm, tn, out_dtype, ss=None, inv_width=1.0):
    t, k = x.shape
    n = w.shape[1]
    args = [x, w]
    in_specs = [pl.BlockSpec((tm, k), lambda j, i: (i, 0)), pl.BlockSpec((k, tn), lambda j, i: (0, j))]
    if ss is not None:
        args.append(ss)
        in_specs.append(pl.BlockSpec((tm, ss.shape[1]), lambda j, i: (i, 0)))
    return pl.pallas_call(
        functools.partial(_mm_body, has_scale=ss is not None, inv_width=inv_width),
        grid=(n // tn, t // tm),
        in_specs=in_specs,
        out_specs=pl.BlockSpec((tm, tn), lambda j, i: (i, j)),
        out_shape=jax.ShapeDtypeStruct((t, n), out_dtype),
        scratch_shapes=[pltpu.VMEM((k, tn), BF16)],
        compiler_params=_params(("arbitrary", "arbitrary")),
        name="matmul_ws",
    )(*args)


def _gateup_body(x_ref, w1_ref, w3_ref, o_ref, w1b_ref, w3b_ref):
    @pl.when(pl.program_id(1) == 0)
    def _():
        _cast_weight(w1_ref, w1b_ref)
        _cast_weight(w3_ref, w3b_ref)

    x = x_ref[...]
    a = jnp.dot(x, w1b_ref[...], preferred_element_type=F32)
    b = jnp.dot(x, w3b_ref[...], preferred_element_type=F32)
    o_ref[...] = (_silu(a) * b).astype(o_ref.dtype)


def _gateup(x, w1, w3, *, tm, tn):
    t, k = x.shape
    n = w1.shape[1]
    w_spec = pl.BlockSpec((k, tn), lambda j, i: (0, j))
    return pl.pallas_call(
        _gateup_body,
        grid=(n // tn, t // tm),
        in_specs=[pl.BlockSpec((tm, k), lambda j, i: (i, 0)), w_spec, w_spec],
        out_specs=pl.BlockSpec((tm, tn), lambda j, i: (i, j)),
        out_shape=jax.ShapeDtypeStruct((t, n), BF16),
        scratch_shapes=[pltpu.VMEM((k, tn), BF16), pltpu.VMEM((k, tn), BF16)],
        compiler_params=_params(("arbitrary", "arbitrary")),
        name="ffn_gate_up",
    )(x, w1, w3)


def _ssd_prep_body(dt_ref, dtb_ref, alog_ref, cum_ref, dtt_ref, *, nc):
    a_coef = -jnp.exp(alog_ref[...])
    q = lax.broadcasted_iota(jnp.int32, (SSD_CHUNK, SSD_CHUNK), 0)
    s = lax.broadcasted_iota(jnp.int32, (SSD_CHUNK, SSD_CHUNK), 1)
    lower = (q >= s).astype(F32)
    upper = (q <= s).astype(F32)
    n_fwd = dt_ref.shape[1] // 2
    for c in range(nc):
        x = dt_ref[c * SSD_CHUNK:(c + 1) * SSD_CHUNK, :] + dtb_ref[...]
        dt = jnp.maximum(x, 0.0) + jnp.log1p(jnp.exp(-jnp.abs(x)))
        a = dt * a_coef
        prefix = jnp.dot(lower, a, preferred_element_type=F32, precision=lax.Precision.HIGHEST)
        suffix = jnp.dot(upper, a, preferred_element_type=F32, precision=lax.Precision.HIGHEST)
        cum = jnp.where(s < n_fwd, prefix, suffix)
        cum_ref[c] = cum.T
        dtt_ref[c] = dt.T


def _ssd_prep(zx, dt_bias, a_log, *, row_block0, nseq, l, dt_col_block):
    nc = l // SSD_CHUNK
    h2 = dt_bias.size
    out = jax.ShapeDtypeStruct((nseq, nc, h2, SSD_CHUNK), F32)
    out_spec = pl.BlockSpec((None, nc, h2, SSD_CHUNK), lambda b: (b, 0, 0, 0))
    return pl.pallas_call(
        functools.partial(_ssd_prep_body, nc=nc),
        grid=(nseq,),
        in_specs=[
            pl.BlockSpec((l, h2), lambda b: (row_block0 + b, dt_col_block)),
            pl.BlockSpec((1, h2), lambda b: (0, 0)),
            pl.BlockSpec((1, h2), lambda b: (0, 0)),
        ],
        out_specs=[out_spec, out_spec],
        out_shape=[out, out],
        compiler_params=_params(("arbitrary",)),
        name="ssd_decay_prep",
    )(zx, dt_bias.reshape(1, h2), a_log.reshape(1, h2))


def _ssd_main_body(*refs, l, has_init, emit_state, n_aliased):
    it = iter(refs)
    z_ref, x_ref, b_ref, c_ref, cum_ref, dtt_ref = [next(it) for _ in range(6)]
    cwx_ref, cwb_ref, cwc_ref, cbx_ref, cbb_ref, cbc_ref = [next(it) for _ in range(6)]
    dvec_ref, nw_ref = next(it), next(it)
    s0_ref = next(it) if has_init else None
    for _ in range(n_aliased):
        next(it)
    u_ref, ss_ref = next(it), next(it)
    st_ref = next(it) if emit_state else None
    pad_ref, xs_ref, bs_ref, cs_ref, y_ref, sf_ref, sb_ref = [next(it) for _ in range(7)]

    nc = l // SSD_CHUNK
    g = pl.program_id(1)
    heads = sf_ref.shape[0]
    hp = SSD_HEADDIM
    gw = heads * hp

    def conv_silu(src_ref, w_ref, bias_ref, dst_ref):
        width = src_ref.shape[1]
        pad_ref[0:8, 0:width] = jnp.zeros((8, width), F32)
        pad_ref[l + 8:l + 16, 0:width] = jnp.zeros((8, width), F32)
        pad_ref[8:l + 8, 0:width] = src_ref[...]
        for c in range(nc):
            acc = jnp.broadcast_to(bias_ref[...], (SSD_CHUNK, width))
            for k in range(D_CONV):
                r0 = c * SSD_CHUNK + 8 - D_CONV // 2 + k
                acc = acc + w_ref[k:k + 1, :] * pad_ref[r0:r0 + SSD_CHUNK, 0:width]
            dst_ref[c * SSD_CHUNK:(c + 1) * SSD_CHUNK, :] = _silu(acc)

    conv_silu(x_ref, cwx_ref, cbx_ref, xs_ref)
    conv_silu(b_ref, cwb_ref, cbb_ref, bs_ref)
    conv_silu(c_ref, cwc_ref, cbc_ref, cs_ref)

    if has_init:
        sf_ref[...] = s0_ref[0]
        sb_ref[...] = s0_ref[1]
    else:
        sf_ref[...] = jnp.zeros(sf_ref.shape, F32)
        sb_ref[...] = jnp.zeros(sb_ref.shape, F32)

    qi = lax.broadcasted_iota(jnp.int32, (SSD_CHUNK, SSD_CHUNK), 0)
    si = lax.broadcasted_iota(jnp.int32, (SSD_CHUNK, SSD_CHUNK), 1)
    lower = qi >= si
    upper = qi <= si
    neg_inf = jnp.float32(-jnp.inf)
    g8 = pl.multiple_of(g * heads, heads)
    n_fwd = cum_ref.shape[1] // 2
    contract_last = (((1,), (1,)), ((), ()))

    def chunk_operands(c):
        rows = pl.ds(pl.multiple_of(c * SSD_CHUNK, SSD_CHUNK), SSD_CHUNK)
        cf_rows = cum_ref[c, pl.ds(g8, heads), :]
        cb_rows = cum_ref[c, pl.ds(n_fwd + g8, heads), :]
        filler = jnp.zeros((SSD_CHUNK - 2 * heads, SSD_CHUNK), F32)
        cols = jnp.concatenate([cf_rows, cb_rows, filler], axis=0).T
        xc = xs_ref[rows, :]
        return rows, cf_rows, cb_rows, cols, xc, xc.T, bs_ref[rows, :].astype(BF16), cs_ref[rows, :].astype(BF16)

    def fwd_chunk(c, carry):
        rows, cf_rows, cb_rows, cols, xc, xt, bc, cc = chunk_operands(c)
        dtf_rows = dtt_ref[c, pl.ds(g8, heads), :]
        dtb_rows = dtt_ref[c, pl.ds(n_fwd + g8, heads), :]
        cb_mat = lax.dot_general(cc, bc, contract_last, preferred_element_type=F32)
        dvec = dvec_ref[...]
        for e in range(heads):
            cf_row, cf_col = cf_rows[e:e + 1, :], cols[:, e:e + 1]
            cb_row, cb_col = cb_rows[e:e + 1, :], cols[:, heads + e:heads + e + 1]
            lf = jnp.exp(jnp.where(lower, cf_col - cf_row, neg_inf))
            lb = jnp.exp(jnp.where(upper, cb_col - cb_row, neg_inf))
            m = cb_mat * (lf * dtf_rows[e:e + 1, :] + lb * dtb_rows[e:e + 1, :])
            hs = slice(e * hp, (e + 1) * hp)
            xh = xc[:, hs]
            y = jnp.dot(m.astype(BF16), xh.astype(BF16), preferred_element_type=F32)
            sf = sf_ref[e]
            y_off = lax.dot_general(cc, sf.astype(BF16), contract_last, preferred_element_type=F32)
            y_ref[rows, hs] = y + jnp.exp(cf_col) * y_off + dvec[:, hs] * xh
            tot = cf_row[:, SSD_CHUNK - 1:SSD_CHUNK]
            wf = jnp.exp(tot - cf_row) * dtf_rows[e:e + 1, :]
            st = jnp.dot((xt[hs, :] * wf).astype(BF16), bc, preferred_element_type=F32)
            sf_ref[e] = jnp.exp(tot) * sf + st
        return carry

    lax.fori_loop(0, nc, fwd_chunk, 0)

    def bwd_chunk(i, carry):
        c = nc - 1 - i
        rows, _, cb_rows, cols, _, xt, bc, cc = chunk_operands(c)
        dtb_rows = dtt_ref[c, pl.ds(n_fwd + g8, heads), :]
        for e in range(heads):
            cb_row, cb_col = cb_rows[e:e + 1, :], cols[:, heads + e:heads + e + 1]
            hs = slice(e * hp, (e + 1) * hp)
            sb = sb_ref[e]
            y_off = lax.dot_general(cc, sb.astype(BF16), contract_last, preferred_element_type=F32)
            y_ref[rows, hs] = y_ref[rows, hs] + jnp.exp(cb_col) * y_off
            tot = cb_row[:, 0:1]
            wb = jnp.exp(tot - cb_row) * dtb_rows[e:e + 1, :]
            st = jnp.dot((xt[hs, :] * wb).astype(BF16), bc, preferred_element_type=F32)
            sb_ref[e] = jnp.exp(tot) * sb + st
        return carry

    lax.fori_loop(0, nc, bwd_chunk, 0)

    if emit_state:
        st_ref[0] = sf_ref[...]
        st_ref[1] = sb_ref[...]

    def gate_chunk(c, carry):
        rows = pl.ds(pl.multiple_of(c * SSD_CHUNK, SSD_CHUNK), SSD_CHUNK)
        v = y_ref[rows, :] * _silu(z_ref[rows, :])
        v2 = v * v
        part = v2[:, 0:128]
        for k in range(1, gw // 128):
            part = part + v2[:, k * 128:(k + 1) * 128]

        @pl.when(g == 0)
        def _():
            ss_ref[rows, :] = part

        @pl.when(g != 0)
        def _():
            ss_ref[rows, :] = ss_ref[rows, :] + part

        u_ref[rows, :] = (v * nw_ref[...]).astype(BF16)
        return carry

    lax.fori_loop(0, nc, gate_chunk, 0)


def _ssd_main(zx, cum, dtt, conv_w, conv_b, dvec, norm_w, s0, bufs, state_out, *,
              row_block0, nseq, l, layer, d_inner):
    t = zx.shape[0]
    heads = d_inner // SSD_HEADDIM // SSD_GROUPS
    gw = heads * SSD_HEADDIM
    n = SSD_STATE
    zb = d_inner // gw
    bb = 2 * d_inner // n
    cb = bb + SSD_GROUPS
    dtb = cb + SSD_GROUPS
    nc = l // SSD_CHUNK
    has_init = s0 is not None
    emit_state = state_out is not None

    def rb(b):
        return row_block0 + b

    in_specs = [
        pl.BlockSpec((l, gw), lambda b, g: (rb(b), g)),
        pl.BlockSpec((l, gw), lambda b, g: (rb(b), zb + g)),
        pl.BlockSpec((l, n), lambda b, g: (rb(b), bb + g)),
        pl.BlockSpec((l, n), lambda b, g: (rb(b), cb + g)),
        pl.BlockSpec((None, nc, cum.shape[2], SSD_CHUNK), lambda b, g: (b, 0, 0, 0)),
        pl.BlockSpec((None, nc, cum.shape[2], SSD_CHUNK), lambda b, g: (b, 0, 0, 0)),
        pl.BlockSpec((D_CONV, gw), lambda b, g: (0, g)),
        pl.BlockSpec((D_CONV, n), lambda b, g: (0, d_inner // n + g)),
        pl.BlockSpec((D_CONV, n), lambda b, g: (0, d_inner // n + SSD_GROUPS + g)),
        pl.BlockSpec((1, gw), lambda b, g: (0, g)),
        pl.BlockSpec((1, n), lambda b, g: (0, d_inner // n + g)),
        pl.BlockSpec((1, n), lambda b, g: (0, d_inner // n + SSD_GROUPS + g)),
        pl.BlockSpec((1, gw), lambda b, g: (0, g)),
        pl.BlockSpec((1, gw), lambda b, g: (0, g)),
    ]
    args = [zx, zx, zx, zx, cum, dtt, conv_w, conv_w, conv_w, conv_b, conv_b, conv_b, dvec, norm_w]
    if has_init:
        in_specs.append(pl.BlockSpec((None, None, 2, heads, SSD_HEADDIM, n), lambda b, g: (b, layer, 0, g, 0, 0)))
        args.append(s0)
    aliases = {}
    out_shape = [jax.ShapeDtypeStruct((t, d_inner), BF16), jax.ShapeDtypeStruct((t, 128), F32)]
    out_specs = [pl.BlockSpec((l, gw), lambda b, g: (rb(b), g)), pl.BlockSpec((l, 128), lambda b, g: (rb(b), 0))]
    any_spec = pl.BlockSpec(memory_space=pl.ANY)

    def alias(buf, out_idx):
        aliases[len(args)] = out_idx
        in_specs.append(any_spec)
        args.append(buf)

    if bufs is not None:
        alias(bufs[0], 0)
        alias(bufs[1], 1)
    if emit_state:
        if not isinstance(state_out, jax.ShapeDtypeStruct):
            alias(state_out, 2)
        out_shape.append(jax.ShapeDtypeStruct(state_out.shape, F32))
        out_specs.append(pl.BlockSpec((None, None, 2, heads, SSD_HEADDIM, n), lambda b, g: (b, layer, 0, g, 0, 0)))
    return pl.pallas_call(
        functools.partial(_ssd_main_body, l=l, has_init=has_init, emit_state=emit_state, n_aliased=len(aliases)),
        grid=(nseq, SSD_GROUPS),
        in_specs=in_specs,
        out_specs=out_specs,
        out_shape=out_shape,
        input_output_aliases=aliases,
        scratch_shapes=[
            pltpu.VMEM((l + 16, gw), F32),
            pltpu.VMEM((l, gw), F32),
            pltpu.VMEM((l, n), F32),
            pltpu.VMEM((l, n), F32),
            pltpu.VMEM((l, gw), F32),
            pltpu.VMEM((heads, SSD_HEADDIM, n), F32),
            pltpu.VMEM((heads, SSD_HEADDIM, n), F32),
        ],
        compiler_params=_params(("arbitrary", "arbitrary")),
        name="ssd_scan_gate",
    )(*args)


def _attn_body(*refs, l, tq, has_cache, rope, emit_kv, scale, n_aliased):
    it = iter(refs)
    q_ref, k_ref, v_ref, qn_ref, kn_ref = [next(it) for _ in range(5)]
    cos_ref, sin_ref = (next(it), next(it)) if rope else (None, None)
    ck_ref, cv_ref = (next(it), next(it)) if has_cache else (None, None)
    for _ in range(n_aliased):
        next(it)
    o_ref = next(it)
    ko_ref, vo_ref = (next(it), next(it)) if emit_kv else (None, None)
    kb_ref = next(it)

    lane = lax.broadcasted_iota(jnp.int32, (1, HEAD_DIM), 1)
    even = (lane % 2) == 0
    contract_last = (((1,), (1,)), ((), ()))

    def norm_rope(x, w, rows):
        r = lax.rsqrt(jnp.mean(x * x, axis=-1, keepdims=True) + EPS)
        xn = (x * r) * w
        if rope:
            partner = jnp.where(even, pltpu.roll(xn, HEAD_DIM - 1, 1), pltpu.roll(xn, 1, 1))
            xn = xn * cos_ref[rows, :] + partner * sin_ref[rows, :]
        return xn

    for c in range(l // tq):
        rows = slice(c * tq, (c + 1) * tq)
        kn = norm_rope(k_ref[rows, :], kn_ref[...], rows)
        kb_ref[rows, :] = kn.astype(BF16)
        if emit_kv:
            ko_ref[rows, :] = kn
    v = v_ref[...]
    if emit_kv:
        vo_ref[...] = v
    vb = v.astype(BF16)
    kb = kb_ref[...]
    if has_cache:
        ckb = ck_ref[...].astype(BF16)
        cvb = cv_ref[...].astype(BF16)

    for c in range(l // tq):
        rows = slice(c * tq, (c + 1) * tq)
        for hh in range(GQA_GROUP):
            cols = slice(hh * HEAD_DIM, (hh + 1) * HEAD_DIM)
            qn = norm_rope(q_ref[rows, cols], qn_ref[...], rows).astype(BF16)
            s = lax.dot_general(qn, kb, contract_last, preferred_element_type=F32) * scale
            mx = jnp.max(s, axis=-1, keepdims=True)
            if has_cache:
                sc = lax.dot_general(qn, ckb, contract_last, preferred_element_type=F32) * scale
                mx = jnp.maximum(mx, jnp.max(sc, axis=-1, keepdims=True))
            p = jnp.exp(s - mx)
            den = jnp.sum(p, axis=-1, keepdims=True)
            o = jnp.dot(p.astype(BF16), vb, preferred_element_type=F32)
            if has_cache:
                pc = jnp.exp(sc - mx)
                den = den + jnp.sum(pc, axis=-1, keepdims=True)
                o = o + jnp.dot(pc.astype(BF16), cvb, preferred_element_type=F32)
            o_ref[rows, cols] = (o / den).astype(o_ref.dtype)


def _attention(qkv, q_norm, k_norm, tables, caches, o_buf, kv_out, *, row_block0, nseq, l, layer, n_heads, n_kv):
    t = qkv.shape[0]
    gq = GQA_GROUP * HEAD_DIM
    kcol = n_heads
    vcol = n_heads + n_kv
    tq = min(l, 256)

    def rb(b):
        return row_block0 + b

    in_specs = [
        pl.BlockSpec((l, gq), lambda b, j: (rb(b), j)),
        pl.BlockSpec((l, HEAD_DIM), lambda b, j: (rb(b), kcol + j)),
        pl.BlockSpec((l, HEAD_DIM), lambda b, j: (rb(b), vcol + j)),
        pl.BlockSpec((1, HEAD_DIM), lambda b, j: (0, 0)),
        pl.BlockSpec((1, HEAD_DIM), lambda b, j: (0, 0)),
    ]
    args = [qkv, qkv, qkv, q_norm.reshape(1, HEAD_DIM), k_norm.reshape(1, HEAD_DIM)]
    if tables is not None:
        in_specs += [pl.BlockSpec((l, HEAD_DIM), lambda b, j: (0, 0))] * 2
        args += list(tables)
    if caches is not None:
        past = caches[0].shape[2]
        in_specs += [pl.BlockSpec((None, None, past, HEAD_DIM), lambda b, j: (b, layer, 0, j))] * 2
        args += list(caches)
    any_spec = pl.BlockSpec(memory_space=pl.ANY)
    aliases = {}

    def alias(buf, out_idx):
        aliases[len(args)] = out_idx
        in_specs.append(any_spec)
        args.append(buf)

    if o_buf is not None:
        alias(o_buf, 0)
    out_shape = [jax.ShapeDtypeStruct((t, n_heads * HEAD_DIM), BF16)]
    out_specs = [pl.BlockSpec((l, gq), lambda b, j: (rb(b), j))]
    if kv_out is not None:
        fresh = isinstance(kv_out, jax.ShapeDtypeStruct)
        for k in range(2):
            if not fresh:
                alias(kv_out[k], 1 + k)
            out_shape.append(jax.ShapeDtypeStruct(kv_out.shape if fresh else kv_out[k].shape, F32))
            out_specs.append(pl.BlockSpec((None, None, l, HEAD_DIM), lambda b, j: (b, layer, 0, j)))
    return pl.pallas_call(
        functools.partial(_attn_body, l=l, tq=tq, has_cache=caches is not None, rope=tables is not None,
                          emit_kv=kv_out is not None, scale=HEAD_DIM ** -0.5, n_aliased=len(aliases)),
        grid=(nseq, n_kv),
        in_specs=in_specs,
        out_specs=out_specs,
        out_shape=out_shape,
        input_output_aliases=aliases,
        scratch_shapes=[pltpu.VMEM((l, HEAD_DIM), BF16)],
        compiler_params=_params(("arbitrary", "arbitrary")),
        name="gqa_attention",
    )(*args)


def _rope_tables(n_tok):
    pairs = HEAD_DIM // 4
    rows = n_tok // GRID_W
    row_id = jnp.broadcast_to(jnp.arange(rows, dtype=F32)[:, None], (rows, GRID_W)).reshape(-1)
    col_id = jnp.broadcast_to(jnp.arange(GRID_W, dtype=F32)[None, :], (rows, GRID_W)).reshape(-1)
    inv = ROPE_THETA ** (-jnp.arange(pairs, dtype=F32) / pairs)
    ang = jnp.concatenate([row_id[:, None] * inv, col_id[:, None] * inv], axis=-1)
    cos, sin = jnp.cos(ang), jnp.sin(ang)
    cos_full = jnp.repeat(cos, 2, axis=-1)
    sin_signed = jnp.stack([-sin, sin], axis=-1).reshape(n_tok, HEAD_DIM)
    return cos_full, sin_signed


def kernel(x_prompt, x_sample, c, state_ssd, cache_k, cache_v, c_ctx, ada_w, ada_b, norm_pre, norm_post, ffn_w1, ffn_w3, ffn_w2, ssd_in_w, ssd_conv_w, ssd_conv_b, ssd_dt_bias, ssd_a_log, ssd_d, ssd_norm_w, ssd_out_w, attn_qkv_w, attn_q_norm, attn_k_norm, attn_o_w):
    batch, seq, d = x_prompt.shape
    dec_batch, dec_seq, _ = x_sample.shape
    depth = ada_w.shape[0]
    n_ctx = batch * seq
    t = n_ctx + dec_batch * dec_seq
    d_inner = ssd_out_w.shape[1]
    n_ssd_heads = ssd_d.shape[1]
    n_kv = cache_k.shape[3]
    n_heads = attn_o_w.shape[1] // HEAD_DIM
    n_ssd_layers = ssd_in_w.shape[0]
    n_attn_layers = attn_qkv_w.shape[0]
    past = cache_k.shape[2]
    assert dec_seq % ROW_TILE == 0 and n_ctx % ROW_TILE == 0 and dec_batch + 1 <= N_COND_ROWS
    ctx_rb0, lat_rb0 = 0, n_ctx // dec_seq

    x = jnp.concatenate([x_prompt.reshape(n_ctx, d), x_sample.reshape(dec_batch * dec_seq, d)], axis=0)
    cond = jnp.concatenate([c_ctx[None, :], c, jnp.zeros((N_COND_ROWS - 1 - dec_batch, d), F32)], axis=0)
    modr = _modulation(cond, ada_w, ada_b).reshape(depth * N_COND_ROWS * N_MOD, 1, d)

    def mod_base(l, j):
        return l * N_COND_ROWS * N_MOD + 3 * j

    def postpre(x, y, post, pre):
        post_a = None if post is None else (norm_post[post[0], post[1]].reshape(1, d), mod_base(*post[:2]), post[2])
        pre_a = None if pre is None else (norm_pre[pre[0], pre[1]].reshape(1, d), mod_base(*pre))
        return _postpre(x, y, modr, n_ctx, dec_seq, post_a, pre_a)

    def ffn(h, l, k):
        a = _gateup(h, ffn_w1[l, k], ffn_w3[l, k], tm=ROW_TILE, tn=512)
        return _mm(a, ffn_w2[l, k], tm=512, tn=512, out_dtype=F32)

    cos_sin = _rope_tables(dec_seq)
    cache_k2 = cache_k.reshape(dec_batch, n_attn_layers, past, n_kv * HEAD_DIM)
    cache_v2 = cache_v.reshape(dec_batch, n_attn_layers, past, n_kv * HEAD_DIM)
    new_kv = jax.ShapeDtypeStruct((batch, n_attn_layers, seq, n_kv * HEAD_DIM), F32)
    new_state = jax.ShapeDtypeStruct((batch, n_ssd_layers, 2, n_ssd_heads, SSD_HEADDIM, SSD_STATE), F32)
    n_mix = depth // n_attn_layers

    (h,) = postpre(x, None, None, (0, 0))
    for l in range(depth):
        y = ffn(h, l, 0)
        x, h = postpre(x, y, (l, 0, 0.5), (l, 1))
        i = l // n_mix
        if l % n_mix == 0:
            zx = _mm(h, ssd_in_w[i], tm=ROW_TILE, tn=1152, out_dtype=F32)
            dt_col = zx.shape[1] // (2 * n_ssd_heads) - 1
            dvec = jnp.repeat(ssd_d[i], SSD_HEADDIM).reshape(1, d_inner)
            nw = ssd_norm_w[i].reshape(1, d_inner)
            conv_b = ssd_conv_b[i].reshape(1, -1)
            cum, dtt = _ssd_prep(zx, ssd_dt_bias[i], ssd_a_log[i], row_block0=ctx_rb0, nseq=batch, l=seq,
                                 dt_col_block=dt_col)
            u, ss, new_state = _ssd_main(zx, cum, dtt, ssd_conv_w[i], conv_b, dvec, nw, None, None, new_state,
                                         row_block0=ctx_rb0, nseq=batch, l=seq, layer=i, d_inner=d_inner)
            cum, dtt = _ssd_prep(zx, ssd_dt_bias[i], ssd_a_log[i], row_block0=lat_rb0, nseq=dec_batch, l=dec_seq,
                                 dt_col_block=dt_col)
            u, ss = _ssd_main(zx, cum, dtt, ssd_conv_w[i], conv_b, dvec, nw, state_ssd, (u, ss), None,
                              row_block0=lat_rb0, nseq=dec_batch, l=dec_seq, layer=i, d_inner=d_inner)
            y = _mm(u, ssd_out_w[i], tm=ROW_TILE, tn=512, out_dtype=F32, ss=ss, inv_width=1.0 / d_inner)
        else:
            qkv = _mm(h, attn_qkv_w[i], tm=ROW_TILE, tn=1024, out_dtype=F32)
            o, *new_kv = _attention(qkv, attn_q_norm[i], attn_k_norm[i], None, None, None, new_kv,
                                    row_block0=ctx_rb0, nseq=batch, l=seq, layer=i, n_heads=n_heads, n_kv=n_kv)
            (o,) = _attention(qkv, attn_q_norm[i], attn_k_norm[i], cos_sin, (cache_k2, cache_v2), o, None,
                              row_block0=lat_rb0, nseq=dec_batch, l=dec_seq, layer=i, n_heads=n_heads, n_kv=n_kv)
            y = _mm(o, attn_o_w[i], tm=ROW_TILE, tn=1024, out_dtype=F32)
        x, h = postpre(x, y, (l, 1, 1.0), (l, 2))
        y = ffn(h, l, 1)
        if l + 1 < depth:
            x, h = postpre(x, y, (l, 2, 0.5), (l + 1, 0))
        else:
            (x,) = postpre(x, y, (l, 2, 0.5), None)

    y_prompt = x[:n_ctx].reshape(batch, seq, d)
    y_sample = x[n_ctx:].reshape(dec_batch, dec_seq, d)
    new_cache_k = new_kv[0].reshape(batch, n_attn_layers, seq, n_kv, HEAD_DIM)
    new_cache_v = new_kv[1].reshape(batch, n_attn_layers, seq, n_kv, HEAD_DIM)
    return (y_prompt, y_sample, new_state.astype(x_prompt.dtype), new_cache_k, new_cache_v)
```

```python
import functools

import jax
import jax.numpy as jnp
from jax import lax
from jax.experimental import pallas as pl
from jax.experimental.pallas import tpu as pltpu

F32 = jnp.float32
BF16 = jnp.bfloat16

N_MOD = 9
N_COND_ROWS = 8
SSD_HEADDIM = 64
SSD_GROUPS = 8
SSD_STATE = 128
SSD_CHUNK = 128
D_CONV = 5
HEAD_DIM = 128
GQA_GROUP = 4
GRID_W = 64
ROPE_THETA = 10000.0
EPS = 1e-6

V7X_VMEM_LIMIT_BYTES = 56 * 1024 * 1024
ROW_TILE = 1024
NORM_ROW_TILE = 256
W_CAST_ROWS = 256
N_SPLIT = 3


def _params(sem):
    return pltpu.CompilerParams(dimension_semantics=sem, vmem_limit_bytes=V7X_VMEM_LIMIT_BYTES)


def _silu(x):
    return x * jax.nn.sigmoid(x)


def _cast_weight(w_ref, wb_ref):
    k = w_ref.shape[0]
    rows = W_CAST_ROWS if k % W_CAST_ROWS == 0 else k

    def body(r, carry):
        sl = pl.ds(pl.multiple_of(r * rows, rows), rows)
        wb_ref[sl, :] = w_ref[sl, :].astype(BF16)
        return carry

    lax.fori_loop(0, k // rows, body, 0)


def _weight_spec(w, widx, k, tn):
    lead = tuple(widx)
    return pl.BlockSpec((None,) * len(lead) + (k, tn), lambda j, i: lead + (0, j))


def _mod_body(c_ref, w_ref, b_ref, o_ref):
    s = _silu(c_ref[...]).astype(BF16)
    o_ref[...] = jnp.dot(s, w_ref[...].astype(BF16), preferred_element_type=F32) + b_ref[...]


def _modulation(cond, ada_w, ada_b):
    depth, d, n = ada_w.shape
    tn = 1024
    return pl.pallas_call(
        _mod_body,
        grid=(depth, n // tn),
        in_specs=[
            pl.BlockSpec((N_COND_ROWS, d), lambda l, j: (0, 0)),
            pl.BlockSpec((None, d, tn), lambda l, j: (l, 0, j)),
            pl.BlockSpec((None, 1, tn), lambda l, j: (l, 0, j)),
        ],
        out_specs=pl.BlockSpec((None, N_COND_ROWS, tn), lambda l, j: (l, 0, j)),
        out_shape=jax.ShapeDtypeStruct((depth, N_COND_ROWS, n), F32),
        compiler_params=_params(("arbitrary", "arbitrary")),
        name="adaln_modulation",
    )(cond, ada_w, ada_b.reshape(depth, 1, n))


def _postpre_body(*refs, has_post, has_pre, res_w, n_aliased):
    it = iter(refs)
    x_ref = next(it)
    if has_post:
        y_ref, gpost_ref, gate_ref = next(it), next(it), next(it)
    if has_pre:
        gpre_ref, scale_ref, shift_ref = next(it), next(it), next(it)
    for _ in range(n_aliased):
        next(it)
    if has_post:
        xo_ref = next(it)
    if has_pre:
        h_ref = next(it)
    x = x_ref[...]
    if has_post:
        y = y_ref[...]
        r = lax.rsqrt(jnp.mean(y * y, axis=-1, keepdims=True) + EPS)
        x = x + (res_w * gate_ref[...]) * ((y * r) * gpost_ref[...])
        xo_ref[...] = x
    if has_pre:
        r = lax.rsqrt(jnp.mean(x * x, axis=-1, keepdims=True) + EPS)
        h = ((x * r) * gpre_ref[...]) * (1.0 + scale_ref[...]) + shift_ref[...]
        h_ref[...] = h.astype(BF16)


def _postpre(x, y, h_buf, modr, t, *, row0, cond_row0, rows_per_cond, post, pre):
    rows, d = x.shape
    tm = NORM_ROW_TILE
    blk0 = row0 // tm

    def mod_spec(base, k):
        return pl.BlockSpec((None, 1, d),
                            lambda i: (base + (cond_row0 + (i * tm) // rows_per_cond) * N_MOD + k, 0, 0))

    own_spec = pl.BlockSpec((tm, d), lambda i: (i, 0))
    tok_spec = pl.BlockSpec((tm, d), lambda i: (blk0 + i, 0))
    vec_spec = pl.BlockSpec((1, d), lambda i: (0, 0))
    args, in_specs, out_shape, out_specs, aliases = [x], [own_spec], [], [], {}
    res_w = 0.0
    if post is not None:
        g_post, base, res_w = post
        args += [y, g_post, modr]
        in_specs += [tok_spec, vec_spec, mod_spec(base, 2)]
        out_shape.append(jax.ShapeDtypeStruct((rows, d), F32))
        out_specs.append(own_spec)
    if pre is not None:
        g_pre, base = pre
        args += [g_pre, modr, modr]
        in_specs += [vec_spec, mod_spec(base, 1), mod_spec(base, 0)]
        if h_buf is not None:
            aliases[len(args)] = len(out_shape)
            args.append(h_buf)
            in_specs.append(pl.BlockSpec(memory_space=pl.ANY))
        out_shape.append(jax.ShapeDtypeStruct((t, d), BF16))
        out_specs.append(tok_spec)
    return pl.pallas_call(
        functools.partial(_postpre_body, has_post=post is not None, has_pre=pre is not None, res_w=res_w,
                          n_aliased=len(aliases)),
        grid=(rows // tm,),
        in_specs=in_specs,
        out_specs=out_specs,
        out_shape=out_shape,
        input_output_aliases=aliases,
        compiler_params=_params(("arbitrary",)),
        name="norm_modulate_residual",
    )(*args)


def _mm_body(*refs, has_scale, inv_width):
    if has_scale:
        x_ref, w_ref, ss_ref, o_ref, wb_ref = refs
    else:
        x_ref, w_ref, o_ref, wb_ref = refs

    @pl.when(pl.program_id(1) == 0)
    def _():
        _cast_weight(w_ref, wb_ref)

    acc = jnp.dot(x_ref[...], wb_ref[...], preferred_element_type=F32)
    if has_scale:
        ms = jnp.sum(ss_ref[...], axis=-1, keepdims=True) * inv_width
        acc = acc * lax.rsqrt(ms + EPS)
    o_ref[...] = acc.astype(o_ref.dtype)


def _mm(x, w, widx, *, tm, tn, out_dtype, ss=None, inv_width=1.0):
    t, k = x.shape
    n = w.shape[-1]
    args = [x, w]
    in_specs = [pl.BlockSpec((tm, k), lambda j, i: (i, 0)), _weight_spec(w, widx, k, tn)]
    if ss is not None:
        args.append(ss)
        in_specs.append(pl.BlockSpec((tm, ss.shape[1]), lambda j, i: (i, 0)))
    return pl.pallas_call(
        functools.partial(_mm_body, has_scale=ss is not None, inv_width=inv_width),
        grid=(n // tn, t // tm),
        in_specs=in_specs,
        out_specs=pl.BlockSpec((tm, tn), lambda j, i: (i, j)),
        out_shape=jax.ShapeDtypeStruct((t, n), out_dtype),
        scratch_shapes=[pltpu.VMEM((k, tn), BF16)],
        compiler_params=_params(("arbitrary", "arbitrary")),
        name="matmul_ws",
    )(*args)


def _gateup_body(x_ref, w1_ref, w3_ref, o_ref, w1b_ref, w3b_ref):
    @pl.when(pl.program_id(1) == 0)
    def _():
        _cast_weight(w1_ref, w1b_ref)
        _cast_weight(w3_ref, w3b_ref)

    x = x_ref[...]
    a = jnp.dot(x, w1b_ref[...], preferred_element_type=F32)
    b = jnp.dot(x, w3b_ref[...], preferred_element_type=F32)
    o_ref[...] = (_silu(a) * b).astype(o_ref.dtype)


def _gateup(x, w1, w3, widx, *, tm, tn):
    t, k = x.shape
    n = w1.shape[-1]
    return pl.pallas_call(
        _gateup_body,
        grid=(n // tn, t // tm),
        in_specs=[pl.BlockSpec((tm, k), lambda j, i: (i, 0)), _weight_spec(w1, widx, k, tn),
                  _weight_spec(w3, widx, k, tn)],
        out_specs=pl.BlockSpec((tm, tn), lambda j, i: (i, j)),
        out_shape=jax.ShapeDtypeStruct((t, n), BF16),
        scratch_shapes=[pltpu.VMEM((k, tn), BF16), pltpu.VMEM((k, tn), BF16)],
        compiler_params=_params(("arbitrary", "arbitrary")),
        name="ffn_gate_up",
    )(x, w1, w3)


def _ssd_prep_body(dt_ref, dtb_ref, alog_ref, cum_ref, dtt_ref, *, nc):
    a_coef = -jnp.exp(alog_ref[...])
    q = lax.broadcasted_iota(jnp.int32, (SSD_CHUNK, SSD_CHUNK), 0)
    s = lax.broadcasted_iota(jnp.int32, (SSD_CHUNK, SSD_CHUNK), 1)
    lower = (q >= s).astype(F32)
    upper = (q <= s).astype(F32)
    n_fwd = dt_ref.shape[1] // 2
    for c in range(nc):
        x = dt_ref[c * SSD_CHUNK:(c + 1) * SSD_CHUNK, :] + dtb_ref[...]
        dt = jnp.maximum(x, 0.0) + jnp.log1p(jnp.exp(-jnp.abs(x)))
        a = dt * a_coef
        prefix = jnp.dot(lower, a, preferred_element_type=F32, precision=lax.Precision.HIGHEST)
        suffix = jnp.dot(upper, a, preferred_element_type=F32, precision=lax.Precision.HIGHEST)
        cum = jnp.where(s < n_fwd, prefix, suffix)
        cum_ref[c] = cum.T
        dtt_ref[c] = dt.T


def _ssd_prep(zx, dt_bias, a_log, *, row_block0, nseq, l, dt_col_block):
    nc = l // SSD_CHUNK
    h2 = dt_bias.size
    out = jax.ShapeDtypeStruct((nseq, nc, h2, SSD_CHUNK), F32)
    out_spec = pl.BlockSpec((None, nc, h2, SSD_CHUNK), lambda b: (b, 0, 0, 0))
    return pl.pallas_call(
        functools.partial(_ssd_prep_body, nc=nc),
        grid=(nseq,),
        in_specs=[
            pl.BlockSpec((l, h2), lambda b: (row_block0 + b, dt_col_block)),
            pl.BlockSpec((1, h2), lambda b: (0, 0)),
            pl.BlockSpec((1, h2), lambda b: (0, 0)),
        ],
        out_specs=[out_spec, out_spec],
        out_shape=[out, out],
        compiler_params=_params(("arbitrary",)),
        name="ssd_decay_prep",
    )(zx, dt_bias.reshape(1, h2), a_log.reshape(1, h2))


def _ssd_main_body(*refs, l, has_init, emit_state, n_aliased):
    it = iter(refs)
    z_ref, x_ref, b_ref, c_ref, cum_ref, dtt_ref = [next(it) for _ in range(6)]
    cwx_ref, cwb_ref, cwc_ref, cbx_ref, cbb_ref, cbc_ref = [next(it) for _ in range(6)]
    dvec_ref, nw_ref, rep_ref = next(it), next(it), next(it)
    s0_ref = next(it) if has_init else None
    for _ in range(n_aliased):
        next(it)
    u_ref, ss_ref = next(it), next(it)
    st_ref = next(it) if emit_state else None
    pad_ref, xs_ref, bs_ref, cs_ref, sloc_ref, sprev_ref, run_ref = [next(it) for _ in range(7)]

    nc = l // SSD_CHUNK
    g = pl.program_id(1)
    hp = SSD_HEADDIM
    gw = xs_ref.shape[1]
    heads = gw // hp
    pw = 2 * hp
    npairs = heads // 2
    n_fwd = cum_ref.shape[1] // 2
    g8 = pl.multiple_of(g * heads, heads)
    contract_last = (((1,), (1,)), ((), ()))

    def conv_silu(src_ref, w_ref, bias_ref, dst_ref):
        width = src_ref.shape[1]
        pad_ref[0:8, 0:width] = jnp.zeros((8, width), F32)
        pad_ref[l + 8:l + 16, 0:width] = jnp.zeros((8, width), F32)
        pad_ref[8:l + 8, 0:width] = src_ref[...]
        win_rows = SSD_CHUNK + 16
        for c in range(nc):
            win = pad_ref[c * SSD_CHUNK:c * SSD_CHUNK + win_rows, 0:width]
            acc = bias_ref[...] + w_ref[D_CONV // 2:D_CONV // 2 + 1, :] * win[8:8 + SSD_CHUNK]
            for k in range(D_CONV):
                shift = k - D_CONV // 2
                if shift != 0:
                    moved = pltpu.roll(win, (-shift) % win_rows, 0)
                    acc = acc + w_ref[k:k + 1, :] * moved[8:8 + SSD_CHUNK]
            dst_ref[c * SSD_CHUNK:(c + 1) * SSD_CHUNK, :] = _silu(acc)

    conv_silu(x_ref, cwx_ref, cbx_ref, xs_ref)
    conv_silu(b_ref, cwb_ref, cbb_ref, bs_ref)
    conv_silu(c_ref, cwc_ref, cbc_ref, cs_ref)

    qi = lax.broadcasted_iota(jnp.int32, (SSD_CHUNK, SSD_CHUNK), 0)
    si = lax.broadcasted_iota(jnp.int32, (SSD_CHUNK, SSD_CHUNK), 1)
    lower, strict_lower, diag = qi >= si, qi > si, qi == si
    left = si < hp
    lane_head = lax.broadcasted_iota(jnp.int32, (1, gw), 1) // hp

    def chunk_rows(c):
        return pl.ds(pl.multiple_of(c * SSD_CHUNK, SSD_CHUNK), SSD_CHUNK)

    def decay_rows(c):
        cf = cum_ref[c, pl.ds(g8, heads), :]
        cb = cum_ref[c, pl.ds(n_fwd + g8, heads), :]
        return cf, cb, dtt_ref[c, pl.ds(g8, heads), :], dtt_ref[c, pl.ds(n_fwd + g8, heads), :]

    def pair_blockdiag(xc, j):
        xp = xc[:, j * pw:(j + 1) * pw]
        zero = jnp.zeros_like(xp)
        return jnp.concatenate([jnp.where(left, xp, zero).astype(BF16), jnp.where(left, zero, xp).astype(BF16)], axis=0)

    def per_head_lanes(col):
        out = jnp.zeros((1, gw), F32)
        for e in range(heads):
            out = jnp.where(lane_head == e, col[e:e + 1, :], out)
        return out

    def local_states(c, carry):
        rows = chunk_rows(c)
        xc = xs_ref[rows, :]
        bt = bs_ref[rows, :].T
        cf, cb, dtf, dtb = decay_rows(c)
        wf = jnp.exp(cf[:, SSD_CHUNK - 1:SSD_CHUNK] - cf) * dtf
        wb = jnp.exp(cb[:, 0:1] - cb) * dtb
        for j in range(npairs):
            xbd = pair_blockdiag(xc, j)
            for d, w in ((0, wf), (1, wb)):
                lhs = jnp.concatenate([bt * w[2 * j:2 * j + 1, :], bt * w[2 * j + 1:2 * j + 2, :]], axis=1)
                sloc_ref[c, d, :, j * pw:(j + 1) * pw] = jnp.dot(lhs.astype(BF16), xbd, preferred_element_type=F32)
        return carry

    lax.fori_loop(0, nc, local_states, 0, unroll=2)

    if has_init:
        for d in range(2):
            for j in range(npairs):
                pair = jnp.concatenate([s0_ref[d, 2 * j], s0_ref[d, 2 * j + 1]], axis=0)
                run_ref[d, :, j * pw:(j + 1) * pw] = pair.T
    else:
        run_ref[...] = jnp.zeros(run_ref.shape, F32)

    def scan_step(c, d, tot):
        s = run_ref[d]
        sprev_ref[c, d] = s.astype(BF16)
        run_ref[d] = s * per_head_lanes(jnp.exp(tot)) + sloc_ref[c, d]

    def fwd_scan(c, carry):
        scan_step(c, 0, cum_ref[c, pl.ds(g8, heads), :][:, SSD_CHUNK - 1:SSD_CHUNK])
        return carry

    def bwd_scan(i, carry):
        c = nc - 1 - i
        scan_step(c, 1, cum_ref[c, pl.ds(n_fwd + g8, heads), :][:, 0:1])
        return carry

    lax.fori_loop(0, nc, fwd_scan, 0)
    lax.fori_loop(0, nc, bwd_scan, 0)

    if emit_state:
        for d in range(2):
            for j in range(npairs):
                pair = run_ref[d, :, j * pw:(j + 1) * pw].T
                st_ref[d, 2 * j] = pair[0:hp]
                st_ref[d, 2 * j + 1] = pair[hp:2 * hp]

    def out_chunk(c, carry):
        rows = chunk_rows(c)
        xc = xs_ref[rows, :]
        cc = cs_ref[rows, :].astype(BF16)
        bc = bs_ref[rows, :].astype(BF16)
        cbm = lax.dot_general(cc, bc, contract_last, preferred_element_type=F32)
        cf, cb, dtf, dtb = decay_rows(c)
        ex = jnp.exp(jnp.concatenate([cf, cb], axis=0))
        pieces, rem = [], ex
        for _ in range(N_SPLIT):
            p = rem.astype(BF16).astype(F32)
            pieces.append(p)
            rem = rem - p
        filler = jnp.zeros((SSD_CHUNK - 2 * heads * (1 + N_SPLIT), SSD_CHUNK), F32)
        cols = jnp.concatenate([cf, cb] + pieces + [filler], axis=0).T
        ecols = jnp.where(si >= 2 * heads, cols, 0.0).astype(BF16)
        e_all = jnp.dot(ecols, rep_ref[...], preferred_element_type=F32)
        y = dvec_ref[...] * xc
        y = y + e_all[:, 0:gw] * jnp.dot(cc, sprev_ref[c, 0], preferred_element_type=F32)
        y = y + e_all[:, gw:2 * gw] * jnp.dot(cc, sprev_ref[c, 1], preferred_element_type=F32)
        zc = z_ref[rows, :]
        nw = nw_ref[...]
        part = jnp.zeros((SSD_CHUNK, pw), F32)
        for j in range(npairs):
            ms = []
            for e in (2 * j, 2 * j + 1):
                arg = jnp.where(lower, cols[:, e:e + 1] - cf[e:e + 1, :],
                                cols[:, heads + e:heads + e + 1] - cb[e:e + 1, :])
                dtf_e, dtb_e = dtf[e:e + 1, :], dtb[e:e + 1, :]
                coef = jnp.where(strict_lower, dtf_e, jnp.where(diag, dtf_e + dtb_e, dtb_e))
                ms.append(cbm * (jnp.exp(arg) * coef))
            lhs = jnp.concatenate(ms, axis=1).astype(BF16)
            cols_j = slice(j * pw, (j + 1) * pw)
            yj = y[:, cols_j] + jnp.dot(lhs, pair_blockdiag(xc, j), preferred_element_type=F32)
            v = yj * _silu(zc[:, cols_j])
            part = part + v * v
            u_ref[rows, cols_j] = (v * nw[:, cols_j]).astype(BF16)

        @pl.when(g == 0)
        def _():
            ss_ref[rows, :] = part

        @pl.when(g != 0)
        def _():
            ss_ref[rows, :] = ss_ref[rows, :] + part

        return carry

    lax.fori_loop(0, nc, out_chunk, 0, unroll=2)


def _ssd_main(zx, cum, dtt, conv_w, widx, conv_b, dvec, norm_w, rep, s0, bufs, state_out, *,
              row_block0, nseq, l, layer, d_inner):
    t = zx.shape[0]
    heads = d_inner // SSD_HEADDIM // SSD_GROUPS
    gw = heads * SSD_HEADDIM
    n = SSD_STATE
    zb = d_inner // gw
    bb = 2 * d_inner // n
    cb = bb + SSD_GROUPS
    nc = l // SSD_CHUNK
    has_init = s0 is not None
    emit_state = state_out is not None

    def rb(b):
        return row_block0 + b

    def lay(width, off):
        return pl.BlockSpec((None, 1, width), lambda b, g: (layer, 0, off + g))

    def cw(width, off):
        return pl.BlockSpec((None, D_CONV, width), lambda b, g: (widx, 0, off + g))

    in_specs = [
        pl.BlockSpec((l, gw), lambda b, g: (rb(b), g)),
        pl.BlockSpec((l, gw), lambda b, g: (rb(b), zb + g)),
        pl.BlockSpec((l, n), lambda b, g: (rb(b), bb + g)),
        pl.BlockSpec((l, n), lambda b, g: (rb(b), cb + g)),
        pl.BlockSpec((None, nc, cum.shape[2], SSD_CHUNK), lambda b, g: (b, 0, 0, 0)),
        pl.BlockSpec((None, nc, cum.shape[2], SSD_CHUNK), lambda b, g: (b, 0, 0, 0)),
        cw(gw, 0), cw(n, d_inner // n), cw(n, d_inner // n + SSD_GROUPS),
        lay(gw, 0), lay(n, d_inner // n), lay(n, d_inner // n + SSD_GROUPS),
        lay(gw, 0),
        lay(gw, 0),
        pl.BlockSpec(rep.shape, lambda b, g: (0, 0)),
    ]
    args = [zx, zx, zx, zx, cum, dtt, conv_w, conv_w, conv_w, conv_b, conv_b, conv_b, dvec, norm_w, rep]
    if has_init:
        in_specs.append(pl.BlockSpec((None, None, 2, heads, SSD_HEADDIM, n), lambda b, g: (b, layer, 0, g, 0, 0)))
        args.append(s0)
    aliases = {}
    out_shape = [jax.ShapeDtypeStruct((t, d_inner), BF16), jax.ShapeDtypeStruct((t, 128), F32)]
    out_specs = [pl.BlockSpec((l, gw), lambda b, g: (rb(b), g)), pl.BlockSpec((l, 128), lambda b, g: (rb(b), 0))]
    any_spec = pl.BlockSpec(memory_space=pl.ANY)

    def alias(buf, out_idx):
        aliases[len(args)] = out_idx
        in_specs.append(any_spec)
        args.append(buf)

    if bufs is not None:
        alias(bufs[0], 0)
        alias(bufs[1], 1)
    if emit_state:
        if not isinstance(state_out, jax.ShapeDtypeStruct):
            alias(state_out, 2)
        out_shape.append(jax.ShapeDtypeStruct(state_out.shape, F32))
        out_specs.append(pl.BlockSpec((None, None, 2, heads, SSD_HEADDIM, n), lambda b, g: (b, layer, 0, g, 0, 0)))
    return pl.pallas_call(
        functools.partial(_ssd_main_body, l=l, has_init=has_init, emit_state=emit_state, n_aliased=len(aliases)),
        grid=(nseq, SSD_GROUPS),
        in_specs=in_specs,
        out_specs=out_specs,
        out_shape=out_shape,
        input_output_aliases=aliases,
        scratch_shapes=[
            pltpu.VMEM((l + 16, gw), F32),
            pltpu.VMEM((l, gw), F32),
            pltpu.VMEM((l, n), F32),
            pltpu.VMEM((l, n), F32),
            pltpu.VMEM((nc, 2, n, gw), F32),
            pltpu.VMEM((nc, 2, n, gw), BF16),
            pltpu.VMEM((2, n, gw), F32),
        ],
        compiler_params=_params(("arbitrary", "arbitrary")),
        name="ssd_scan_gate",
    )(*args)


def _head_lane_spreader(heads, gw):
    hd2 = 2 * heads
    rows = jnp.arange(SSD_CHUNK)[:, None]
    lanes = jnp.arange(2 * gw)[None, :]
    hit = (rows >= hd2) & (rows < hd2 * (1 + N_SPLIT)) & ((rows % hd2) == lanes // SSD_HEADDIM)
    return hit.astype(BF16)


def _attn_body(*refs, l, tq, has_cache, rope, emit_kv, scale, n_aliased):
    it = iter(refs)
    q_ref, k_ref, v_ref, qn_ref, kn_ref = [next(it) for _ in range(5)]
    cos_ref, sin_ref = (next(it), next(it)) if rope else (None, None)
    ck_ref, cv_ref = (next(it), next(it)) if has_cache else (None, None)
    for _ in range(n_aliased):
        next(it)
    o_ref = next(it)
    ko_ref, vo_ref = (next(it), next(it)) if emit_kv else (None, None)
    kb_ref = next(it)

    lane = lax.broadcasted_iota(jnp.int32, (1, HEAD_DIM), 1)
    even = (lane % 2) == 0
    contract_last = (((1,), (1,)), ((), ()))

    def norm_rope(x, w, rows):
        r = lax.rsqrt(jnp.mean(x * x, axis=-1, keepdims=True) + EPS)
        xn = (x * r) * w
        if rope:
            partner = jnp.where(even, pltpu.roll(xn, HEAD_DIM - 1, 1), pltpu.roll(xn, 1, 1))
            xn = xn * cos_ref[rows, :] + partner * sin_ref[rows, :]
        return xn

    for c in range(l // tq):
        rows = slice(c * tq, (c + 1) * tq)
        kn = norm_rope(k_ref[rows, :], kn_ref[...], rows)
        kb_ref[rows, :] = kn.astype(BF16)
        if emit_kv:
            ko_ref[rows, :] = kn
    v = v_ref[...]
    if emit_kv:
        vo_ref[...] = v
    vb = v.astype(BF16)
    kb = kb_ref[...]
    if has_cache:
        ckb = ck_ref[...].astype(BF16)
        cvb = cv_ref[...].astype(BF16)

    for c in range(l // tq):
        rows = slice(c * tq, (c + 1) * tq)
        for hh in range(GQA_GROUP):
            cols = slice(hh * HEAD_DIM, (hh + 1) * HEAD_DIM)
            qn = norm_rope(q_ref[rows, cols], qn_ref[...], rows).astype(BF16)
            s = lax.dot_general(qn, kb, contract_last, preferred_element_type=F32) * scale
            mx = jnp.max(s, axis=-1, keepdims=True)
            if has_cache:
                sc = lax.dot_general(qn, ckb, contract_last, preferred_element_type=F32) * scale
                mx = jnp.maximum(mx, jnp.max(sc, axis=-1, keepdims=True))
            p = jnp.exp(s - mx)
            den = jnp.sum(p, axis=-1, keepdims=True)
            o = jnp.dot(p.astype(BF16), vb, preferred_element_type=F32)
            if has_cache:
                pc = jnp.exp(sc - mx)
                den = den + jnp.sum(pc, axis=-1, keepdims=True)
                o = o + jnp.dot(pc.astype(BF16), cvb, preferred_element_type=F32)
            o_ref[rows, cols] = (o / den).astype(o_ref.dtype)


def _attention(qkv, q_norm, k_norm, tables, caches, o_buf, kv_out, *, row_block0, nseq, l, layer, n_heads, n_kv):
    t = qkv.shape[0]
    gq = GQA_GROUP * HEAD_DIM
    kcol = n_heads
    vcol = n_heads + n_kv
    tq = min(l, 256)

    def rb(b):
        return row_block0 + b

    norm_spec = pl.BlockSpec((None, 1, HEAD_DIM), lambda b, j: (layer, 0, 0))
    in_specs = [
        pl.BlockSpec((l, gq), lambda b, j: (rb(b), j)),
        pl.BlockSpec((l, HEAD_DIM), lambda b, j: (rb(b), kcol + j)),
        pl.BlockSpec((l, HEAD_DIM), lambda b, j: (rb(b), vcol + j)),
        norm_spec, norm_spec,
    ]
    args = [qkv, qkv, qkv, q_norm, k_norm]
    if tables is not None:
        in_specs += [pl.BlockSpec((l, HEAD_DIM), lambda b, j: (0, 0))] * 2
        args += list(tables)
    if caches is not None:
        past = caches[0].shape[2]
        in_specs += [pl.BlockSpec((None, None, past, HEAD_DIM), lambda b, j: (b, layer, 0, j))] * 2
        args += list(caches)
    any_spec = pl.BlockSpec(memory_space=pl.ANY)
    aliases = {}

    def alias(buf, out_idx):
        aliases[len(args)] = out_idx
        in_specs.append(any_spec)
        args.append(buf)

    if o_buf is not None:
        alias(o_buf, 0)
    out_shape = [jax.ShapeDtypeStruct((t, n_heads * HEAD_DIM), BF16)]
    out_specs = [pl.BlockSpec((l, gq), lambda b, j: (rb(b), j))]
    if kv_out is not None:
        fresh = isinstance(kv_out, jax.ShapeDtypeStruct)
        for k in range(2):
            if not fresh:
                alias(kv_out[k], 1 + k)
            out_shape.append(jax.ShapeDtypeStruct(kv_out.shape if fresh else kv_out[k].shape, F32))
            out_specs.append(pl.BlockSpec((None, None, l, HEAD_DIM), lambda b, j: (b, layer, 0, j)))
    return pl.pallas_call(
        functools.partial(_attn_body, l=l, tq=tq, has_cache=caches is not None, rope=tables is not None,
                          emit_kv=kv_out is not None, scale=HEAD_DIM ** -0.5, n_aliased=len(aliases)),
        grid=(nseq, n_kv),
        in_specs=in_specs,
        out_specs=out_specs,
        out_shape=out_shape,
        input_output_aliases=aliases,
        scratch_shapes=[pltpu.VMEM((l, HEAD_DIM), BF16)],
        compiler_params=_params(("arbitrary", "arbitrary")),
        name="gqa_attention",
    )(*args)


def _rope_tables(n_tok):
    pairs = HEAD_DIM // 4
    rows = n_tok // GRID_W
    row_id = jnp.broadcast_to(jnp.arange(rows, dtype=F32)[:, None], (rows, GRID_W)).reshape(-1)
    col_id = jnp.broadcast_to(jnp.arange(GRID_W, dtype=F32)[None, :], (rows, GRID_W)).reshape(-1)
    inv = ROPE_THETA ** (-jnp.arange(pairs, dtype=F32) / pairs)
    ang = jnp.concatenate([row_id[:, None] * inv, col_id[:, None] * inv], axis=-1)
    cos, sin = jnp.cos(ang), jnp.sin(ang)
    cos_full = jnp.repeat(cos, 2, axis=-1)
    sin_signed = jnp.stack([-sin, sin], axis=-1).reshape(n_tok, HEAD_DIM)
    return cos_full, sin_signed


def kernel(x_prompt, x_sample, c, state_ssd, cache_k, cache_v, c_ctx, ada_w, ada_b, norm_pre, norm_post, ffn_w1, ffn_w3, ffn_w2, ssd_in_w, ssd_conv_w, ssd_conv_b, ssd_dt_bias, ssd_a_log, ssd_d, ssd_norm_w, ssd_out_w, attn_qkv_w, attn_q_norm, attn_k_norm, attn_o_w):
    batch, seq, d = x_prompt.shape
    dec_batch, dec_seq, _ = x_sample.shape
    depth = ada_w.shape[0]
    n_ctx = batch * seq
    n_lat = dec_batch * dec_seq
    t = n_ctx + n_lat
    d_inner = ssd_out_w.shape[1]
    n_ssd_heads = ssd_d.shape[1]
    n_kv = cache_k.shape[3]
    n_heads = attn_o_w.shape[1] // HEAD_DIM
    n_ssd_layers = ssd_in_w.shape[0]
    n_attn_layers = attn_qkv_w.shape[0]
    past = cache_k.shape[2]
    assert dec_seq % ROW_TILE == 0 and n_ctx % ROW_TILE == 0 and dec_batch + 1 <= N_COND_ROWS
    ctx_rb0, lat_rb0 = 0, n_ctx // dec_seq

    cond = jnp.concatenate([c_ctx[None, :], c, jnp.zeros((N_COND_ROWS - 1 - dec_batch, d), F32)], axis=0)
    modr = _modulation(cond, ada_w, ada_b).reshape(depth * N_COND_ROWS * N_MOD, 1, d)
    norm_pre_r = norm_pre.reshape(depth, 3, 1, d)
    norm_post_r = norm_post.reshape(depth, 3, 1, d)

    def postpre(xs, y, post, pre):
        post_a = pre_a = None
        if post is not None:
            l, j, res_w = post
            post_a = (norm_post_r[l, j], (l * N_COND_ROWS) * N_MOD + 3 * j, res_w)
        if pre is not None:
            l, j = pre
            pre_a = (norm_pre_r[l, j], (l * N_COND_ROWS) * N_MOD + 3 * j)
        out_c = _postpre(xs[0], y, None, modr, t, row0=0, cond_row0=0, rows_per_cond=n_ctx, post=post_a, pre=pre_a)
        h = out_c[-1] if pre is not None else None
        out_l = _postpre(xs[1], y, h, modr, t, row0=n_ctx, cond_row0=1, rows_per_cond=dec_seq, post=post_a, pre=pre_a)
        new_xs = (out_c[0], out_l[0]) if post is not None else xs
        return new_xs, (out_l[-1] if pre is not None else None)

    def ffn(h, l, k):
        a = _gateup(h, ffn_w1, ffn_w3, (l, k), tm=ROW_TILE, tn=512)
        return _mm(a, ffn_w2, (l, k), tm=512, tn=512, out_dtype=F32)

    cos_sin = _rope_tables(dec_seq)
    cache_k2 = cache_k.reshape(dec_batch, n_attn_layers, past, n_kv * HEAD_DIM)
    cache_v2 = cache_v.reshape(dec_batch, n_attn_layers, past, n_kv * HEAD_DIM)
    q_norm_r = attn_q_norm.reshape(n_attn_layers, 1, HEAD_DIM)
    k_norm_r = attn_k_norm.reshape(n_attn_layers, 1, HEAD_DIM)
    new_kv = jax.ShapeDtypeStruct((batch, n_attn_layers, seq, n_kv * HEAD_DIM), F32)
    new_state = jax.ShapeDtypeStruct((batch, n_ssd_layers, 2, n_ssd_heads, SSD_HEADDIM, SSD_STATE), F32)
    conv_b_r = ssd_conv_b.reshape(n_ssd_layers, 1, -1)
    dvec_r = jnp.repeat(ssd_d, SSD_HEADDIM, axis=-1).reshape(n_ssd_layers, 1, d_inner)
    ssd_nw_r = ssd_norm_w.reshape(n_ssd_layers, 1, d_inner)
    group_heads = n_ssd_heads // SSD_GROUPS
    rep = _head_lane_spreader(group_heads, group_heads * SSD_HEADDIM)
    n_mix = depth // n_attn_layers

    xs = (x_prompt.reshape(n_ctx, d), x_sample.reshape(n_lat, d))
    _, h = postpre(xs, None, None, (0, 0))
    for l in range(depth):
        y = ffn(h, l, 0)
        xs, h = postpre(xs, y, (l, 0, 0.5), (l, 1))
        i = l // n_mix
        if l % n_mix == 0:
            zx = _mm(h, ssd_in_w, (i,), tm=ROW_TILE, tn=1152, out_dtype=F32)
            dt_col = zx.shape[1] // (2 * n_ssd_heads) - 1
            cum, dtt = _ssd_prep(zx, ssd_dt_bias[i], ssd_a_log[i], row_block0=ctx_rb0, nseq=batch, l=seq,
                                 dt_col_block=dt_col)
            u, ss, new_state = _ssd_main(zx, cum, dtt, ssd_conv_w, i, conv_b_r, dvec_r, ssd_nw_r, rep, None, None,
                                         new_state, row_block0=ctx_rb0, nseq=batch, l=seq, layer=i, d_inner=d_inner)
            cum, dtt = _ssd_prep(zx, ssd_dt_bias[i], ssd_a_log[i], row_block0=lat_rb0, nseq=dec_batch, l=dec_seq,
                                 dt_col_block=dt_col)
            u, ss = _ssd_main(zx, cum, dtt, ssd_conv_w, i, conv_b_r, dvec_r, ssd_nw_r, rep, state_ssd, (u, ss),
                              None, row_block0=lat_rb0, nseq=dec_batch, l=dec_seq, layer=i, d_inner=d_inner)
            y = _mm(u, ssd_out_w, (i,), tm=ROW_TILE, tn=512, out_dtype=F32, ss=ss, inv_width=1.0 / d_inner)
        else:
            qkv = _mm(h, attn_qkv_w, (i,), tm=ROW_TILE, tn=1024, out_dtype=F32)
            o, *new_kv = _attention(qkv, q_norm_r, k_norm_r, None, None, None, new_kv,
                                    row_block0=ctx_rb0, nseq=batch, l=seq, layer=i, n_heads=n_heads, n_kv=n_kv)
            (o,) = _attention(qkv, q_norm_r, k_norm_r, cos_sin, (cache_k2, cache_v2), o, None,
                              row_block0=lat_rb0, nseq=dec_batch, l=dec_seq, layer=i, n_heads=n_heads, n_kv=n_kv)
            y = _mm(o, attn_o_w, (i,), tm=ROW_TILE, tn=1024, out_dtype=F32)
        xs, h = postpre(xs, y, (l, 1, 1.0), (l, 2))
        y = ffn(h, l, 1)
        xs, h = postpre(xs, y, (l, 2, 0.5), (l + 1, 0) if l + 1 < depth else None)

    y_prompt = xs[0].reshape(batch, seq, d)
    y_sample = xs[1].reshape(dec_batch, dec_seq, d)
    new_cache_k = new_kv[0].reshape(batch, n_attn_layers, seq, n_kv, HEAD_DIM)
    new_cache_v = new_kv[1].reshape(batch, n_attn_layers, seq, n_kv, HEAD_DIM)
    return (y_prompt, y_sample, new_state.astype(x_prompt.dtype), new_cache_k, new_cache_v)
```

```python
import functools

import jax
import jax.numpy as jnp
from jax import lax
from jax.experimental import pallas as pl
from jax.experimental.pallas import tpu as pltpu

F32 = jnp.float32
BF16 = jnp.bfloat16

N_MOD = 9
N_COND_ROWS = 8
SSD_HEADDIM = 64
SSD_GROUPS = 8
SSD_STATE = 128
SSD_CHUNK = 128
D_CONV = 5
HEAD_DIM = 128
GQA_GROUP = 4
GRID_W = 64
ROPE_THETA = 10000.0
EPS = 1e-6

V7X_VMEM_LIMIT_BYTES = 56 * 1024 * 1024
ROW_TILE = 1024
NORM_ROW_TILE = 512
W_CAST_ROWS = 256
CONV_COL_CHUNK = 256
N_SPLIT = 3


def _params(sem):
    return pltpu.CompilerParams(dimension_semantics=sem, vmem_limit_bytes=V7X_VMEM_LIMIT_BYTES)


def _silu(x):
    h = 0.5 * x
    return h + h * jnp.tanh(h)


def _cast_weight(w_ref, wb_ref):
    k = w_ref.shape[0]
    rows = W_CAST_ROWS if k % W_CAST_ROWS == 0 else k

    def body(r, carry):
        sl = pl.ds(pl.multiple_of(r * rows, rows), rows)
        wb_ref[sl, :] = w_ref[sl, :].astype(BF16)
        return carry

    lax.fori_loop(0, k // rows, body, 0)


def _weight_spec(w, widx, k, tn, col0=0):
    lead = tuple(widx)
    return pl.BlockSpec((None,) * len(lead) + (k, tn), lambda j, i: lead + (0, col0 + j))


def _mod_body(c_ref, w_ref, b_ref, o_ref):
    s = _silu(c_ref[...]).astype(BF16)
    o_ref[...] = jnp.dot(s, w_ref[...].astype(BF16), preferred_element_type=F32) + b_ref[...]


def _modulation(cond, ada_w, ada_b):
    depth, d, n = ada_w.shape
    tn = 1024
    return pl.pallas_call(
        _mod_body,
        grid=(depth, n // tn),
        in_specs=[
            pl.BlockSpec((N_COND_ROWS, d), lambda l, j: (0, 0)),
            pl.BlockSpec((None, d, tn), lambda l, j: (l, 0, j)),
            pl.BlockSpec((None, 1, tn), lambda l, j: (l, 0, j)),
        ],
        out_specs=pl.BlockSpec((None, N_COND_ROWS, tn), lambda l, j: (l, 0, j)),
        out_shape=jax.ShapeDtypeStruct((depth, N_COND_ROWS, n), F32),
        compiler_params=_params(("arbitrary", "arbitrary")),
        name="adaln_modulation",
    )(cond, ada_w, ada_b.reshape(depth, 1, n))


def _postpre_body(*refs, has_post, has_pre, res_w, n_aliased):
    it = iter(refs)
    x_ref = next(it)
    if has_post:
        y_ref, gpost_ref, gate_ref = next(it), next(it), next(it)
    if has_pre:
        gpre_ref, scale_ref, shift_ref = next(it), next(it), next(it)
    for _ in range(n_aliased):
        next(it)
    if has_post:
        xo_ref = next(it)
    if has_pre:
        h_ref = next(it)
    x = x_ref[...]
    if has_post:
        y = y_ref[...]
        r = lax.rsqrt(jnp.mean(y * y, axis=-1, keepdims=True) + EPS)
        x = x + (res_w * gate_ref[...]) * ((y * r) * gpost_ref[...])
        xo_ref[...] = x
    if has_pre:
        r = lax.rsqrt(jnp.mean(x * x, axis=-1, keepdims=True) + EPS)
        h = ((x * r) * gpre_ref[...]) * (1.0 + scale_ref[...]) + shift_ref[...]
        h_ref[...] = h.astype(BF16)


def _postpre(x, y, h_buf, modr, t, *, row0, cond_row0, rows_per_cond, post, pre):
    rows, d = x.shape
    tm = NORM_ROW_TILE
    blk0 = row0 // tm

    def mod_spec(base, k):
        return pl.BlockSpec((None, 1, d),
                            lambda i: (base + (cond_row0 + (i * tm) // rows_per_cond) * N_MOD + k, 0, 0))

    own_spec = pl.BlockSpec((tm, d), lambda i: (i, 0))
    tok_spec = pl.BlockSpec((tm, d), lambda i: (blk0 + i, 0))
    vec_spec = pl.BlockSpec((1, d), lambda i: (0, 0))
    args, in_specs, out_shape, out_specs, aliases = [x], [own_spec], [], [], {}
    res_w = 0.0
    if post is not None:
        g_post, base, res_w = post
        args += [y, g_post, modr]
        in_specs += [tok_spec, vec_spec, mod_spec(base, 2)]
        out_shape.append(jax.ShapeDtypeStruct((rows, d), F32))
        out_specs.append(own_spec)
    if pre is not None:
        g_pre, base = pre
        args += [g_pre, modr, modr]
        in_specs += [vec_spec, mod_spec(base, 1), mod_spec(base, 0)]
        if h_buf is not None:
            aliases[len(args)] = len(out_shape)
            args.append(h_buf)
            in_specs.append(pl.BlockSpec(memory_space=pl.ANY))
        out_shape.append(jax.ShapeDtypeStruct((t, d), BF16))
        out_specs.append(tok_spec)
    return pl.pallas_call(
        functools.partial(_postpre_body, has_post=post is not None, has_pre=pre is not None, res_w=res_w,
                          n_aliased=len(aliases)),
        grid=(rows // tm,),
        in_specs=in_specs,
        out_specs=out_specs,
        out_shape=out_shape,
        input_output_aliases=aliases,
        compiler_params=_params(("arbitrary",)),
        name="norm_modulate_residual",
    )(*args)


def _mm_body(*refs, has_scale, inv_width):
    if has_scale:
        x_ref, w_ref, ss_ref, o_ref, wb_ref = refs
    else:
        x_ref, w_ref, o_ref, wb_ref = refs

    @pl.when(pl.program_id(1) == 0)
    def _():
        _cast_weight(w_ref, wb_ref)

    acc = jnp.dot(x_ref[...], wb_ref[...], preferred_element_type=F32)
    if has_scale:
        ms = jnp.sum(ss_ref[...], axis=-1, keepdims=True) * inv_width
        acc = acc * lax.rsqrt(ms + EPS)
    o_ref[...] = acc.astype(o_ref.dtype)


def _mm(x, w, widx, *, tm, tn, out_dtype, ss=None, inv_width=1.0, col0=0, n_out=None):
    t, k = x.shape
    n = w.shape[-1] if n_out is None else n_out
    args = [x, w]
    in_specs = [pl.BlockSpec((tm, k), lambda j, i: (i, 0)), _weight_spec(w, widx, k, tn, col0)]
    if ss is not None:
        args.append(ss)
        in_specs.append(pl.BlockSpec((tm, ss.shape[1]), lambda j, i: (i, 0)))
    return pl.pallas_call(
        functools.partial(_mm_body, has_scale=ss is not None, inv_width=inv_width),
        grid=(n // tn, t // tm),
        in_specs=in_specs,
        out_specs=pl.BlockSpec((tm, tn), lambda j, i: (i, j)),
        out_shape=jax.ShapeDtypeStruct((t, n), out_dtype),
        scratch_shapes=[pltpu.VMEM((k, tn), BF16)],
        compiler_params=_params(("arbitrary", "arbitrary")),
        name="matmul_ws",
    )(*args)


def _gateup_body(x_ref, w1_ref, w3_ref, o_ref, w1b_ref, w3b_ref):
    @pl.when(pl.program_id(1) == 0)
    def _():
        _cast_weight(w1_ref, w1b_ref)
        _cast_weight(w3_ref, w3b_ref)

    x = x_ref[...]
    a = jnp.dot(x, w1b_ref[...], preferred_element_type=F32)
    b = jnp.dot(x, w3b_ref[...], preferred_element_type=F32)
    o_ref[...] = (_silu(a) * b).astype(o_ref.dtype)


def _gateup(x, w1, w3, widx, *, tm, tn):
    t, k = x.shape
    n = w1.shape[-1]
    return pl.pallas_call(
        _gateup_body,
        grid=(n // tn, t // tm),
        in_specs=[pl.BlockSpec((tm, k), lambda j, i: (i, 0)), _weight_spec(w1, widx, k, tn),
                  _weight_spec(w3, widx, k, tn)],
        out_specs=pl.BlockSpec((tm, tn), lambda j, i: (i, j)),
        out_shape=jax.ShapeDtypeStruct((t, n), BF16),
        scratch_shapes=[pltpu.VMEM((k, tn), BF16), pltpu.VMEM((k, tn), BF16)],
        compiler_params=_params(("arbitrary", "arbitrary")),
        name="ffn_gate_up",
    )(x, w1, w3)


def _mm_conv_body(x_ref, w_ref, cw_ref, cb_ref, o_ref, wb_ref, *, seq, n_short_tiles):
    i = pl.program_id(1)

    @pl.when(i == 0)
    def _():
        _cast_weight(w_ref, wb_ref)

    tm, tn = o_ref.shape
    x = x_ref[...]
    keep_halo = (i >= n_short_tiles).astype(F32)
    nseg = tm // seq
    win_rows = seq + 16
    zero8 = jnp.zeros((8, CONV_COL_CHUNK), F32)
    for q in range(tn // CONV_COL_CHUNK):
        cols = slice(q * CONV_COL_CHUNK, (q + 1) * CONV_COL_CHUNK)
        raw = jnp.dot(x, wb_ref[:, cols], preferred_element_type=F32)
        wq = cw_ref[:, cols]
        bq = cb_ref[:, cols]
        for s in range(nseg):
            seg = raw[s * seq:(s + 1) * seq]
            top = raw[s * seq - 8:s * seq] * keep_halo if s > 0 else zero8
            bot = raw[(s + 1) * seq:(s + 1) * seq + 8] * keep_halo if s < nseg - 1 else zero8
            win = jnp.concatenate([top, seg, bot], axis=0)
            acc = bq + wq[D_CONV // 2:D_CONV // 2 + 1, :] * seg
            for k in range(D_CONV):
                shift = k - D_CONV // 2
                if shift != 0:
                    moved = pltpu.roll(win, (-shift) % win_rows, 0)
                    acc = acc + wq[k:k + 1, :] * moved[8:8 + seq]
            o_ref[s * seq:(s + 1) * seq, cols] = _silu(acc)


def _mm_conv(x, w, widx, conv_w, conv_b, layer, *, tm, tn, col0, n_out, seq, n_short_tiles):
    t, k = x.shape
    return pl.pallas_call(
        functools.partial(_mm_conv_body, seq=seq, n_short_tiles=n_short_tiles),
        grid=(n_out // tn, t // tm),
        in_specs=[
            pl.BlockSpec((tm, k), lambda j, i: (i, 0)),
            _weight_spec(w, widx, k, tn, col0),
            pl.BlockSpec((None, D_CONV, tn), lambda j, i: (layer, 0, j)),
            pl.BlockSpec((None, 1, tn), lambda j, i: (layer, 0, j)),
        ],
        out_specs=pl.BlockSpec((tm, tn), lambda j, i: (i, j)),
        out_shape=jax.ShapeDtypeStruct((t, n_out), F32),
        scratch_shapes=[pltpu.VMEM((k, tn), BF16)],
        compiler_params=_params(("arbitrary", "arbitrary")),
        name="matmul_conv_silu",
    )(x, w, conv_w, conv_b)


def _ssd_prep_body(dt_ref, dtb_ref, alog_ref, cum_ref, dtt_ref, *, nc):
    a_coef = -jnp.exp(alog_ref[...])
    q = lax.broadcasted_iota(jnp.int32, (SSD_CHUNK, SSD_CHUNK), 0)
    s = lax.broadcasted_iota(jnp.int32, (SSD_CHUNK, SSD_CHUNK), 1)
    lower = (q >= s).astype(F32)
    upper = (q <= s).astype(F32)
    n_fwd = dt_ref.shape[1] // 2
    for c in range(nc):
        x = dt_ref[c * SSD_CHUNK:(c + 1) * SSD_CHUNK, :] + dtb_ref[...]
        dt = jnp.maximum(x, 0.0) + jnp.log1p(jnp.exp(-jnp.abs(x)))
        a = dt * a_coef
        prefix = jnp.dot(lower, a, preferred_element_type=F32, precision=lax.Precision.HIGHEST)
        suffix = jnp.dot(upper, a, preferred_element_type=F32, precision=lax.Precision.HIGHEST)
        cum = jnp.where(s < n_fwd, prefix, suffix)
        cum_ref[c] = cum.T
        dtt_ref[c] = dt.T


def _ssd_prep(dt_raw, dt_bias, a_log, *, row_block0, nseq, l):
    nc = l // SSD_CHUNK
    h2 = dt_bias.size
    out = jax.ShapeDtypeStruct((nseq, nc, h2, SSD_CHUNK), F32)
    out_spec = pl.BlockSpec((None, nc, h2, SSD_CHUNK), lambda b: (b, 0, 0, 0))
    return pl.pallas_call(
        functools.partial(_ssd_prep_body, nc=nc),
        grid=(nseq,),
        in_specs=[
            pl.BlockSpec((l, h2), lambda b: (row_block0 + b, 0)),
            pl.BlockSpec((1, h2), lambda b: (0, 0)),
            pl.BlockSpec((1, h2), lambda b: (0, 0)),
        ],
        out_specs=[out_spec, out_spec],
        out_shape=[out, out],
        compiler_params=_params(("arbitrary",)),
        name="ssd_decay_prep",
    )(dt_raw, dt_bias.reshape(1, h2), a_log.reshape(1, h2))


def _ssd_main_body(*refs, l, has_init, emit_state, n_aliased):
    it = iter(refs)
    z_ref, xs_ref, bs_ref, cs_ref, cum_ref, dtt_ref = [next(it) for _ in range(6)]
    dvec_ref, nw_ref, rep_ref = next(it), next(it), next(it)
    s0_ref = next(it) if has_init else None
    for _ in range(n_aliased):
        next(it)
    u_ref, ss_ref = next(it), next(it)
    st_ref = next(it) if emit_state else None
    sloc_ref, sprev_ref, run_ref = [next(it) for _ in range(3)]

    nc = l // SSD_CHUNK
    g = pl.program_id(1)
    hp = SSD_HEADDIM
    gw = xs_ref.shape[1]
    heads = gw // hp
    pw = 2 * hp
    npairs = heads // 2
    n_fwd = cum_ref.shape[1] // 2
    g8 = pl.multiple_of(g * heads, heads)
    contract_last = (((1,), (1,)), ((), ()))

    qi = lax.broadcasted_iota(jnp.int32, (SSD_CHUNK, SSD_CHUNK), 0)
    si = lax.broadcasted_iota(jnp.int32, (SSD_CHUNK, SSD_CHUNK), 1)
    lower, strict_lower, diag = qi >= si, qi > si, qi == si
    left = si < hp
    lane_head = lax.broadcasted_iota(jnp.int32, (1, gw), 1) // hp

    def chunk_rows(c):
        return pl.ds(pl.multiple_of(c * SSD_CHUNK, SSD_CHUNK), SSD_CHUNK)

    def decay_rows(c):
        cf = cum_ref[c, pl.ds(g8, heads), :]
        cb = cum_ref[c, pl.ds(n_fwd + g8, heads), :]
        return cf, cb, dtt_ref[c, pl.ds(g8, heads), :], dtt_ref[c, pl.ds(n_fwd + g8, heads), :]

    def pair_blockdiag(xc, j):
        xp = xc[:, j * pw:(j + 1) * pw]
        zero = jnp.zeros_like(xp)
        return jnp.concatenate([jnp.where(left, xp, zero).astype(BF16), jnp.where(left, zero, xp).astype(BF16)], axis=0)

    def per_head_lanes(col):
        out = jnp.zeros((1, gw), F32)
        for e in range(heads):
            out = jnp.where(lane_head == e, col[e:e + 1, :], out)
        return out

    if has_init:
        for d in range(2):
            for j in range(npairs):
                pair = jnp.concatenate([s0_ref[d, 2 * j], s0_ref[d, 2 * j + 1]], axis=0)
                run_ref[d, :, j * pw:(j + 1) * pw] = pair.T
    else:
        run_ref[...] = jnp.zeros(run_ref.shape, F32)

    def local_states(c, carry):
        rows = chunk_rows(c)
        xc = xs_ref[rows, :]
        bt = bs_ref[rows, :].T
        cf, cb, dtf, dtb = decay_rows(c)
        tot_f = cf[:, SSD_CHUNK - 1:SSD_CHUNK]
        wf = jnp.exp(tot_f - cf) * dtf
        wb = jnp.exp(cb[:, 0:1] - cb) * dtb
        dec_f = per_head_lanes(jnp.exp(tot_f))
        for j in range(npairs):
            cols_j = slice(j * pw, (j + 1) * pw)
            xbd = pair_blockdiag(xc, j)

            def contribution(w):
                lhs = jnp.concatenate([bt * w[2 * j:2 * j + 1, :], bt * w[2 * j + 1:2 * j + 2, :]], axis=1)
                return jnp.dot(lhs.astype(BF16), xbd, preferred_element_type=F32)

            sf = run_ref[0, :, cols_j]
            sprev_ref[c, :, cols_j] = sf.astype(BF16)
            run_ref[0, :, cols_j] = sf * dec_f[:, cols_j] + contribution(wf)
            sloc_ref[c, :, cols_j] = contribution(wb)
        return carry

    lax.fori_loop(0, nc, local_states, 0, unroll=2)

    def out_chunk(i, carry):
        c = nc - 1 - i
        rows = chunk_rows(c)
        xc = xs_ref[rows, :]
        cc = cs_ref[rows, :].astype(BF16)
        bc = bs_ref[rows, :].astype(BF16)
        cbm = lax.dot_general(cc, bc, contract_last, preferred_element_type=F32)
        cf, cb, dtf, dtb = decay_rows(c)
        ex = jnp.exp(jnp.concatenate([cf, cb], axis=0))
        pieces, rem = [], ex
        for _ in range(N_SPLIT):
            p = rem.astype(BF16).astype(F32)
            pieces.append(p)
            rem = rem - p
        filler = jnp.zeros((SSD_CHUNK - 2 * heads * (1 + N_SPLIT), SSD_CHUNK), F32)
        cols = jnp.concatenate([cf, cb] + pieces + [filler], axis=0).T
        ecols = jnp.where(si >= 2 * heads, cols, 0.0).astype(BF16)
        e_all = jnp.dot(ecols, rep_ref[...], preferred_element_type=F32)
        sb = run_ref[1]
        y = dvec_ref[...] * xc
        y = y + e_all[:, 0:gw] * jnp.dot(cc, sprev_ref[c], preferred_element_type=F32)
        y = y + e_all[:, gw:2 * gw] * jnp.dot(cc, sb.astype(BF16), preferred_element_type=F32)
        run_ref[1] = sb * per_head_lanes(jnp.exp(cb[:, 0:1])) + sloc_ref[c]
        zc = z_ref[rows, :]
        nw = nw_ref[...]
        part = jnp.zeros((SSD_CHUNK, pw), F32)
        for j in range(npairs):
            ms = []
            for e in (2 * j, 2 * j + 1):
                arg = jnp.where(lower, cols[:, e:e + 1] - cf[e:e + 1, :],
                                cols[:, heads + e:heads + e + 1] - cb[e:e + 1, :])
                dtf_e, dtb_e = dtf[e:e + 1, :], dtb[e:e + 1, :]
                coef = jnp.where(strict_lower, dtf_e, jnp.where(diag, dtf_e + dtb_e, dtb_e))
                ms.append(cbm * (jnp.exp(arg) * coef))
            lhs = jnp.concatenate(ms, axis=1).astype(BF16)
            cols_j = slice(j * pw, (j + 1) * pw)
            yj = y[:, cols_j] + jnp.dot(lhs, pair_blockdiag(xc, j), preferred_element_type=F32)
            v = yj * _silu(zc[:, cols_j])
            part = part + v * v
            u_ref[rows, cols_j] = (v * nw[:, cols_j]).astype(BF16)

        @pl.when(g == 0)
        def _():
            ss_ref[rows, :] = part

        @pl.when(g != 0)
        def _():
            ss_ref[rows, :] = ss_ref[rows, :] + part

        return carry

    lax.fori_loop(0, nc, out_chunk, 0, unroll=2)

    if emit_state:
        for d in range(2):
            for j in range(npairs):
                pair = run_ref[d, :, j * pw:(j + 1) * pw].T
                st_ref[d, 2 * j] = pair[0:hp]
                st_ref[d, 2 * j + 1] = pair[hp:2 * hp]


def _ssd_main(z, xbc, cum, dtt, dvec, norm_w, rep, s0, bufs, state_out, *, row_block0, nseq, l, layer, d_inner):
    t = z.shape[0]
    heads = d_inner // SSD_HEADDIM // SSD_GROUPS
    gw = heads * SSD_HEADDIM
    n = SSD_STATE
    bb = d_inner // n
    cb = bb + SSD_GROUPS
    nc = l // SSD_CHUNK
    has_init = s0 is not None
    emit_state = state_out is not None

    def rb(b):
        return row_block0 + b

    def lay(width, off):
        return pl.BlockSpec((None, 1, width), lambda b, g: (layer, 0, off + g))

    in_specs = [
        pl.BlockSpec((l, gw), lambda b, g: (rb(b), g)),
        pl.BlockSpec((l, gw), lambda b, g: (rb(b), g)),
        pl.BlockSpec((l, n), lambda b, g: (rb(b), bb + g)),
        pl.BlockSpec((l, n), lambda b, g: (rb(b), cb + g)),
        pl.BlockSpec((None, nc, cum.shape[2], SSD_CHUNK), lambda b, g: (b, 0, 0, 0)),
        pl.BlockSpec((None, nc, cum.shape[2], SSD_CHUNK), lambda b, g: (b, 0, 0, 0)),
        lay(gw, 0),
        lay(gw, 0),
        pl.BlockSpec(rep.shape, lambda b, g: (0, 0)),
    ]
    args = [z, xbc, xbc, xbc, cum, dtt, dvec, norm_w, rep]
    if has_init:
        in_specs.append(pl.BlockSpec((None, None, 2, heads, SSD_HEADDIM, n), lambda b, g: (b, layer, 0, g, 0, 0)))
        args.append(s0)
    aliases = {}
    out_shape = [jax.ShapeDtypeStruct((t, d_inner), BF16), jax.ShapeDtypeStruct((t, 128), F32)]
    out_specs = [pl.BlockSpec((l, gw), lambda b, g: (rb(b), g)), pl.BlockSpec((l, 128), lambda b, g: (rb(b), 0))]
    any_spec = pl.BlockSpec(memory_space=pl.ANY)

    def alias(buf, out_idx):
        aliases[len(args)] = out_idx
        in_specs.append(any_spec)
        args.append(buf)

    if bufs is not None:
        alias(bufs[0], 0)
        alias(bufs[1], 1)
    if emit_state:
        if not isinstance(state_out, jax.ShapeDtypeStruct):
            alias(state_out, 2)
        out_shape.append(jax.ShapeDtypeStruct(state_out.shape, F32))
        out_specs.append(pl.BlockSpec((None, None, 2, heads, SSD_HEADDIM, n), lambda b, g: (b, layer, 0, g, 0, 0)))
    return pl.pallas_call(
        functools.partial(_ssd_main_body, l=l, has_init=has_init, emit_state=emit_state, n_aliased=len(aliases)),
        grid=(nseq, SSD_GROUPS),
        in_specs=in_specs,
        out_specs=out_specs,
        out_shape=out_shape,
        input_output_aliases=aliases,
        scratch_shapes=[
            pltpu.VMEM((nc, n, gw), F32),
            pltpu.VMEM((nc, n, gw), BF16),
            pltpu.VMEM((2, n, gw), F32),
        ],
        compiler_params=_params(("arbitrary", "arbitrary")),
        name="ssd_scan_gate",
    )(*args)


def _head_lane_spreader(heads, gw):
    hd2 = 2 * heads
    rows = jnp.arange(SSD_CHUNK)[:, None]
    lanes = jnp.arange(2 * gw)[None, :]
    hit = (rows >= hd2) & (rows < hd2 * (1 + N_SPLIT)) & ((rows % hd2) == lanes // SSD_HEADDIM)
    return hit.astype(BF16)


def _attn_body(*refs, l, tq, has_cache, rope, emit_kv, scale, n_aliased):
    it = iter(refs)
    q_ref, k_ref, v_ref, qn_ref, kn_ref = [next(it) for _ in range(5)]
    cos_ref, sin_ref = (next(it), next(it)) if rope else (None, None)
    ck_ref, cv_ref = (next(it), next(it)) if has_cache else (None, None)
    for _ in range(n_aliased):
        next(it)
    o_ref = next(it)
    ko_ref, vo_ref = (next(it), next(it)) if emit_kv else (None, None)
    kb_ref = next(it)

    lane = lax.broadcasted_iota(jnp.int32, (1, HEAD_DIM), 1)
    even = (lane % 2) == 0
    contract_last = (((1,), (1,)), ((), ()))

    def norm_rope(x, w, rows):
        r = lax.rsqrt(jnp.mean(x * x, axis=-1, keepdims=True) + EPS)
        xn = (x * r) * w
        if rope:
            partner = jnp.where(even, pltpu.roll(xn, HEAD_DIM - 1, 1), pltpu.roll(xn, 1, 1))
            xn = xn * cos_ref[rows, :] + partner * sin_ref[rows, :]
        return xn

    for c in range(l // tq):
        rows = slice(c * tq, (c + 1) * tq)
        kn = norm_rope(k_ref[rows, :], kn_ref[...], rows)
        kb_ref[rows, :] = kn.astype(BF16)
        if emit_kv:
            ko_ref[rows, :] = kn
    v = v_ref[...]
    if emit_kv:
        vo_ref[...] = v
    vb = v.astype(BF16)
    kb = kb_ref[...]
    if has_cache:
        ckb = ck_ref[...].astype(BF16)
        cvb = cv_ref[...].astype(BF16)

    for c in range(l // tq):
        rows = slice(c * tq, (c + 1) * tq)
        for hh in range(GQA_GROUP):
            cols = slice(hh * HEAD_DIM, (hh + 1) * HEAD_DIM)
            qn = norm_rope(q_ref[rows, cols], qn_ref[...], rows).astype(BF16)
            s = lax.dot_general(qn, kb, contract_last, preferred_element_type=F32) * scale
            mx = jnp.max(s, axis=-1, keepdims=True)
            if has_cache:
                sc = lax.dot_general(qn, ckb, contract_last, preferred_element_type=F32) * scale
                mx = jnp.maximum(mx, jnp.max(sc, axis=-1, keepdims=True))
            p = jnp.exp(s - mx)
            den = jnp.sum(p, axis=-1, keepdims=True)
            o = jnp.dot(p.astype(BF16), vb, preferred_element_type=F32)
            if has_cache:
                pc = jnp.exp(sc - mx)
                den = den + jnp.sum(pc, axis=-1, keepdims=True)
                o = o + jnp.dot(pc.astype(BF16), cvb, preferred_element_type=F32)
            o_ref[rows, cols] = (o / den).astype(o_ref.dtype)


def _attention(qkv, q_norm, k_norm, tables, caches, o_buf, kv_out, *, row_block0, nseq, l, layer, n_heads, n_kv):
    t = qkv.shape[0]
    gq = GQA_GROUP * HEAD_DIM
    kcol = n_heads
    vcol = n_heads + n_kv
    tq = min(l, 256)

    def rb(b):
        return row_block0 + b

    norm_spec = pl.BlockSpec((None, 1, HEAD_DIM), lambda b, j: (layer, 0, 0))
    in_specs = [
        pl.BlockSpec((l, gq), lambda b, j: (rb(b), j)),
        pl.BlockSpec((l, HEAD_DIM), lambda b, j: (rb(b), kcol + j)),
        pl.BlockSpec((l, HEAD_DIM), lambda b, j: (rb(b), vcol + j)),
        norm_spec, norm_spec,
    ]
    args = [qkv, qkv, qkv, q_norm, k_norm]
    if tables is not None:
        in_specs += [pl.BlockSpec((l, HEAD_DIM), lambda b, j: (0, 0))] * 2
        args += list(tables)
    if caches is not None:
        past = caches[0].shape[2]
        in_specs += [pl.BlockSpec((None, None, past, HEAD_DIM), lambda b, j: (b, layer, 0, j))] * 2
        args += list(caches)
    any_spec = pl.BlockSpec(memory_space=pl.ANY)
    aliases = {}

    def alias(buf, out_idx):
        aliases[len(args)] = out_idx
        in_specs.append(any_spec)
        args.append(buf)

    if o_buf is not None:
        alias(o_buf, 0)
    out_shape = [jax.ShapeDtypeStruct((t, n_heads * HEAD_DIM), BF16)]
    out_specs = [pl.BlockSpec((l, gq), lambda b, j: (rb(b), j))]
    if kv_out is not None:
        fresh = isinstance(kv_out, jax.ShapeDtypeStruct)
        for k in range(2):
            if not fresh:
                alias(kv_out[k], 1 + k)
            out_shape.append(jax.ShapeDtypeStruct(kv_out.shape if fresh else kv_out[k].shape, F32))
            out_specs.append(pl.BlockSpec((None, None, l, HEAD_DIM), lambda b, j: (b, layer, 0, j)))
    return pl.pallas_call(
        functools.partial(_attn_body, l=l, tq=tq, has_cache=caches is not None, rope=tables is not None,
                          emit_kv=kv_out is not None, scale=HEAD_DIM ** -0.5, n_aliased=len(aliases)),
        grid=(nseq, n_kv),
        in_specs=in_specs,
        out_specs=out_specs,
        out_shape=out_shape,
        input_output_aliases=aliases,
        scratch_shapes=[pltpu.VMEM((l, HEAD_DIM), BF16)],
        compiler_params=_params(("arbitrary", "arbitrary")),
        name="gqa_attention",
    )(*args)


def _rope_tables(n_tok):
    pairs = HEAD_DIM // 4
    rows = n_tok // GRID_W
    row_id = jnp.broadcast_to(jnp.arange(rows, dtype=F32)[:, None], (rows, GRID_W)).reshape(-1)
    col_id = jnp.broadcast_to(jnp.arange(GRID_W, dtype=F32)[None, :], (rows, GRID_W)).reshape(-1)
    inv = ROPE_THETA ** (-jnp.arange(pairs, dtype=F32) / pairs)
    ang = jnp.concatenate([row_id[:, None] * inv, col_id[:, None] * inv], axis=-1)
    cos, sin = jnp.cos(ang), jnp.sin(ang)
    cos_full = jnp.repeat(cos, 2, axis=-1)
    sin_signed = jnp.stack([-sin, sin], axis=-1).reshape(n_tok, HEAD_DIM)
    return cos_full, sin_signed


def kernel(x_prompt, x_sample, c, state_ssd, cache_k, cache_v, c_ctx, ada_w, ada_b, norm_pre, norm_post, ffn_w1, ffn_w3, ffn_w2, ssd_in_w, ssd_conv_w, ssd_conv_b, ssd_dt_bias, ssd_a_log, ssd_d, ssd_norm_w, ssd_out_w, attn_qkv_w, attn_q_norm, attn_k_norm, attn_o_w):
    batch, seq, d = x_prompt.shape
    dec_batch, dec_seq, _ = x_sample.shape
    depth = ada_w.shape[0]
    n_ctx = batch * seq
    n_lat = dec_batch * dec_seq
    t = n_ctx + n_lat
    d_inner = ssd_out_w.shape[1]
    n_ssd_heads = ssd_d.shape[1]
    n_kv = cache_k.shape[3]
    n_heads = attn_o_w.shape[1] // HEAD_DIM
    n_ssd_layers = ssd_in_w.shape[0]
    n_attn_layers = attn_qkv_w.shape[0]
    past = cache_k.shape[2]
    assert dec_seq % ROW_TILE == 0 and n_ctx % ROW_TILE == 0 and dec_batch + 1 <= N_COND_ROWS
    ctx_rb0, lat_rb0 = 0, n_ctx // dec_seq

    cond = jnp.concatenate([c_ctx[None, :], c, jnp.zeros((N_COND_ROWS - 1 - dec_batch, d), F32)], axis=0)
    modr = _modulation(cond, ada_w, ada_b).reshape(depth * N_COND_ROWS * N_MOD, 1, d)
    norm_pre_r = norm_pre.reshape(depth, 3, 1, d)
    norm_post_r = norm_post.reshape(depth, 3, 1, d)

    def postpre(xs, y, post, pre):
        post_a = pre_a = None
        if post is not None:
            l, j, res_w = post
            post_a = (norm_post_r[l, j], (l * N_COND_ROWS) * N_MOD + 3 * j, res_w)
        if pre is not None:
            l, j = pre
            pre_a = (norm_pre_r[l, j], (l * N_COND_ROWS) * N_MOD + 3 * j)
        out_c = _postpre(xs[0], y, None, modr, t, row0=0, cond_row0=0, rows_per_cond=n_ctx, post=post_a, pre=pre_a)
        h = out_c[-1] if pre is not None else None
        out_l = _postpre(xs[1], y, h, modr, t, row0=n_ctx, cond_row0=1, rows_per_cond=dec_seq, post=post_a, pre=pre_a)
        new_xs = (out_c[0], out_l[0]) if post is not None else xs
        return new_xs, (out_l[-1] if pre is not None else None)

    def ffn(h, l, k):
        a = _gateup(h, ffn_w1, ffn_w3, (l, k), tm=ROW_TILE, tn=512)
        return _mm(a, ffn_w2, (l, k), tm=512, tn=512, out_dtype=F32)

    cos_sin = _rope_tables(dec_seq)
    cache_k2 = cache_k.reshape(dec_batch, n_attn_layers, past, n_kv * HEAD_DIM)
    cache_v2 = cache_v.reshape(dec_batch, n_attn_layers, past, n_kv * HEAD_DIM)
    q_norm_r = attn_q_norm.reshape(n_attn_layers, 1, HEAD_DIM)
    k_norm_r = attn_k_norm.reshape(n_attn_layers, 1, HEAD_DIM)
    new_kv = jax.ShapeDtypeStruct((batch, n_attn_layers, seq, n_kv * HEAD_DIM), F32)
    new_state = jax.ShapeDtypeStruct((batch, n_ssd_layers, 2, n_ssd_heads, SSD_HEADDIM, SSD_STATE), F32)
    conv_dim = ssd_conv_w.shape[-1]
    dt_width = 2 * n_ssd_heads
    assert seq % SSD_CHUNK == 0 and ROW_TILE % seq == 0 and dec_seq == ROW_TILE
    assert d_inner % 1024 == 0 and conv_dim % 1024 == 0 and (d_inner + conv_dim) % dt_width == 0
    conv_b_r = ssd_conv_b.reshape(n_ssd_layers, 1, conv_dim)
    dvec_r = jnp.repeat(ssd_d, SSD_HEADDIM, axis=-1).reshape(n_ssd_layers, 1, d_inner)
    ssd_nw_r = ssd_norm_w.reshape(n_ssd_layers, 1, d_inner)
    group_heads = n_ssd_heads // SSD_GROUPS
    rep = _head_lane_spreader(group_heads, group_heads * SSD_HEADDIM)
    n_mix = depth // n_attn_layers

    xs = (x_prompt.reshape(n_ctx, d), x_sample.reshape(n_lat, d))
    _, h = postpre(xs, None, None, (0, 0))
    for l in range(depth):
        y = ffn(h, l, 0)
        xs, h = postpre(xs, y, (l, 0, 0.5), (l, 1))
        i = l // n_mix
        if l % n_mix == 0:
            z = _mm(h, ssd_in_w, (i,), tm=ROW_TILE, tn=1024, out_dtype=F32, n_out=d_inner)
            xbc = _mm_conv(h, ssd_in_w, (i,), ssd_conv_w, conv_b_r, i, tm=ROW_TILE, tn=1024, col0=d_inner // 1024,
                           n_out=conv_dim, seq=seq, n_short_tiles=n_ctx // ROW_TILE)
            dt_raw = _mm(h, ssd_in_w, (i,), tm=ROW_TILE, tn=dt_width, out_dtype=F32,
                         col0=(d_inner + conv_dim) // dt_width, n_out=dt_width)
            cum, dtt = _ssd_prep(dt_raw, ssd_dt_bias[i], ssd_a_log[i], row_block0=ctx_rb0, nseq=batch, l=seq)
            u, ss, new_state = _ssd_main(z, xbc, cum, dtt, dvec_r, ssd_nw_r, rep, None, None, new_state,
                                         row_block0=ctx_rb0, nseq=batch, l=seq, layer=i, d_inner=d_inner)
            cum, dtt = _ssd_prep(dt_raw, ssd_dt_bias[i], ssd_a_log[i], row_block0=lat_rb0, nseq=dec_batch, l=dec_seq)
            u, ss = _ssd_main(z, xbc, cum, dtt, dvec_r, ssd_nw_r, rep, state_ssd, (u, ss), None,
                              row_block0=lat_rb0, nseq=dec_batch, l=dec_seq, layer=i, d_inner=d_inner)
            y = _mm(u, ssd_out_w, (i,), tm=ROW_TILE, tn=512, out_dtype=F32, ss=ss, inv_width=1.0 / d_inner)
        else:
            qkv = _mm(h, attn_qkv_w, (i,), tm=ROW_TILE, tn=1024, out_dtype=F32)
            o, *new_kv = _attention(qkv, q_norm_r, k_norm_r, None, None, None, new_kv,
                                    row_block0=ctx_rb0, nseq=batch, l=seq, layer=i, n_heads=n_heads, n_kv=n_kv)
            (o,) = _attention(qkv, q_norm_r, k_norm_r, cos_sin, (cache_k2, cache_v2), o, None,
                              row_block0=lat_rb0, nseq=dec_batch, l=dec_seq, layer=i, n_heads=n_heads, n_kv=n_kv)
            y = _mm(o, attn_o_w, (i,), tm=ROW_TILE, tn=1024, out_dtype=F32)
        xs, h = postpre(xs, y, (l, 1, 1.0), (l, 2))
        y = ffn(h, l, 1)
        xs, h = postpre(xs, y, (l, 2, 0.5), (l + 1, 0) if l + 1 < depth else None)

    y_prompt = xs[0].reshape(batch, seq, d)
    y_sample = xs[1].reshape(dec_batch, dec_seq, d)
    new_cache_k = new_kv[0].reshape(batch, n_attn_layers, seq, n_kv, HEAD_DIM)
    new_cache_v = new_kv[1].reshape(batch, n_attn_layers, seq, n_kv, HEAD_DIM)
    return (y_prompt, y_sample, new_state.astype(x_prompt.dtype), new_cache_k, new_cache_v)
```

```python
import functools

import jax
import jax.numpy as jnp
from jax import lax
from jax.experimental import pallas as pl
from jax.experimental.pallas import tpu as pltpu

F32 = jnp.float32
BF16 = jnp.bfloat16

N_MOD = 9
N_COND_ROWS = 8
SSD_HEADDIM = 64
SSD_GROUPS = 8
SSD_STATE = 128
SSD_CHUNK = 128
D_CONV = 5
HEAD_DIM = 128
GQA_GROUP = 4
GRID_W = 64
ROPE_THETA = 10000.0
EPS = 1e-6

V7X_VMEM_LIMIT_BYTES = 56 * 1024 * 1024
ROW_TILE = 1024
NORM_ROW_TILE = 512
W_CAST_ROWS = 256
CONV_COL_CHUNK = 256
N_SPLIT = 3


def _params(sem):
    return pltpu.CompilerParams(dimension_semantics=sem, vmem_limit_bytes=V7X_VMEM_LIMIT_BYTES)


def _silu(x):
    h = 0.5 * x
    return h + h * jnp.tanh(h)


def _cast_weight(w_ref, wb_ref):
    k = w_ref.shape[0]
    rows = W_CAST_ROWS if k % W_CAST_ROWS == 0 else k

    def body(r, carry):
        sl = pl.ds(pl.multiple_of(r * rows, rows), rows)
        wb_ref[sl, :] = w_ref[sl, :].astype(BF16)
        return carry

    lax.fori_loop(0, k // rows, body, 0)


def _weight_spec(w, widx, k, tn, col0=0):
    lead = tuple(widx)
    return pl.BlockSpec((None,) * len(lead) + (k, tn), lambda j, i: lead + (0, col0 + j))


def _mod_body(c_ref, w_ref, b_ref, o_ref):
    s = _silu(c_ref[...]).astype(BF16)
    o_ref[...] = jnp.dot(s, w_ref[...].astype(BF16), preferred_element_type=F32) + b_ref[...]


def _modulation(cond, ada_w, ada_b):
    depth, d, n = ada_w.shape
    tn = 1024
    return pl.pallas_call(
        _mod_body,
        grid=(depth, n // tn),
        in_specs=[
            pl.BlockSpec((N_COND_ROWS, d), lambda l, j: (0, 0)),
            pl.BlockSpec((None, d, tn), lambda l, j: (l, 0, j)),
            pl.BlockSpec((None, 1, tn), lambda l, j: (l, 0, j)),
        ],
        out_specs=pl.BlockSpec((None, N_COND_ROWS, tn), lambda l, j: (l, 0, j)),
        out_shape=jax.ShapeDtypeStruct((depth, N_COND_ROWS, n), F32),
        compiler_params=_params(("arbitrary", "arbitrary")),
        name="adaln_modulation",
    )(cond, ada_w, ada_b.reshape(depth, 1, n))


def _postpre_body(*refs, has_post, has_pre, res_w, n_aliased):
    it = iter(refs)
    x_ref = next(it)
    if has_post:
        y_ref, gpost_ref, gate_ref = next(it), next(it), next(it)
    if has_pre:
        gpre_ref, scale_ref, shift_ref = next(it), next(it), next(it)
    for _ in range(n_aliased):
        next(it)
    if has_post:
        xo_ref = next(it)
    if has_pre:
        h_ref = next(it)
    x = x_ref[...]
    if has_post:
        y = y_ref[...]
        r = lax.rsqrt(jnp.mean(y * y, axis=-1, keepdims=True) + EPS)
        x = x + (res_w * gate_ref[...]) * ((y * r) * gpost_ref[...])
        xo_ref[...] = x
    if has_pre:
        r = lax.rsqrt(jnp.mean(x * x, axis=-1, keepdims=True) + EPS)
        h = ((x * r) * gpre_ref[...]) * (1.0 + scale_ref[...]) + shift_ref[...]
        h_ref[...] = h.astype(BF16)


def _postpre(x, y, h_buf, modr, t, *, row0, cond_row0, rows_per_cond, post, pre):
    rows, d = x.shape
    tm = NORM_ROW_TILE
    blk0 = row0 // tm

    def mod_spec(base, k):
        return pl.BlockSpec((None, 1, d),
                            lambda i: (base + (cond_row0 + (i * tm) // rows_per_cond) * N_MOD + k, 0, 0))

    own_spec = pl.BlockSpec((tm, d), lambda i: (i, 0))
    tok_spec = pl.BlockSpec((tm, d), lambda i: (blk0 + i, 0))
    vec_spec = pl.BlockSpec((1, d), lambda i: (0, 0))
    args, in_specs, out_shape, out_specs, aliases = [x], [own_spec], [], [], {}
    res_w = 0.0
    if post is not None:
        g_post, base, res_w = post
        args += [y, g_post, modr]
        in_specs += [tok_spec, vec_spec, mod_spec(base, 2)]
        out_shape.append(jax.ShapeDtypeStruct((rows, d), F32))
        out_specs.append(own_spec)
    if pre is not None:
        g_pre, base = pre
        args += [g_pre, modr, modr]
        in_specs += [vec_spec, mod_spec(base, 1), mod_spec(base, 0)]
        if h_buf is not None:
            aliases[len(args)] = len(out_shape)
            args.append(h_buf)
            in_specs.append(pl.BlockSpec(memory_space=pl.ANY))
        out_shape.append(jax.ShapeDtypeStruct((t, d), BF16))
        out_specs.append(tok_spec)
    return pl.pallas_call(
        functools.partial(_postpre_body, has_post=post is not None, has_pre=pre is not None, res_w=res_w,
                          n_aliased=len(aliases)),
        grid=(rows // tm,),
        in_specs=in_specs,
        out_specs=out_specs,
        out_shape=out_shape,
        input_output_aliases=aliases,
        compiler_params=_params(("arbitrary",)),
        name="norm_modulate_residual",
    )(*args)


def _mm_body(*refs, has_scale, inv_width):
    if has_scale:
        x_ref, w_ref, ss_ref, o_ref, wb_ref = refs
    else:
        x_ref, w_ref, o_ref, wb_ref = refs

    @pl.when(pl.program_id(1) == 0)
    def _():
        _cast_weight(w_ref, wb_ref)

    acc = jnp.dot(x_ref[...], wb_ref[...], preferred_element_type=F32)
    if has_scale:
        ms = jnp.sum(ss_ref[...], axis=-1, keepdims=True) * inv_width
        acc = acc * lax.rsqrt(ms + EPS)
    o_ref[...] = acc.astype(o_ref.dtype)


def _mm(x, w, widx, *, tm, tn, out_dtype, ss=None, inv_width=1.0, col0=0, n_out=None):
    t, k = x.shape
    n = w.shape[-1] if n_out is None else n_out
    args = [x, w]
    in_specs = [pl.BlockSpec((tm, k), lambda j, i: (i, 0)), _weight_spec(w, widx, k, tn, col0)]
    if ss is not None:
        args.append(ss)
        in_specs.append(pl.BlockSpec((tm, ss.shape[1]), lambda j, i: (i, 0)))
    return pl.pallas_call(
        functools.partial(_mm_body, has_scale=ss is not None, inv_width=inv_width),
        grid=(n // tn, t // tm),
        in_specs=in_specs,
        out_specs=pl.BlockSpec((tm, tn), lambda j, i: (i, j)),
        out_shape=jax.ShapeDtypeStruct((t, n), out_dtype),
        scratch_shapes=[pltpu.VMEM((k, tn), BF16)],
        compiler_params=_params(("arbitrary", "arbitrary")),
        name="matmul_ws",
    )(*args)


def _gateup_body(x_ref, w1_ref, w3_ref, o_ref, w1b_ref, w3b_ref):
    @pl.when(pl.program_id(1) == 0)
    def _():
        _cast_weight(w1_ref, w1b_ref)
        _cast_weight(w3_ref, w3b_ref)

    x = x_ref[...]
    a = jnp.dot(x, w1b_ref[...], preferred_element_type=F32)
    b = jnp.dot(x, w3b_ref[...], preferred_element_type=F32)
    o_ref[...] = (_silu(a) * b).astype(o_ref.dtype)


def _gateup(x, w1, w3, widx, *, tm, tn):
    t, k = x.shape
    n = w1.shape[-1]
    return pl.pallas_call(
        _gateup_body,
        grid=(n // tn, t // tm),
        in_specs=[pl.BlockSpec((tm, k), lambda j, i: (i, 0)), _weight_spec(w1, widx, k, tn),
                  _weight_spec(w3, widx, k, tn)],
        out_specs=pl.BlockSpec((tm, tn), lambda j, i: (i, j)),
        out_shape=jax.ShapeDtypeStruct((t, n), BF16),
        scratch_shapes=[pltpu.VMEM((k, tn), BF16), pltpu.VMEM((k, tn), BF16)],
        compiler_params=_params(("arbitrary", "arbitrary")),
        name="ffn_gate_up",
    )(x, w1, w3)


def _mm_conv_body(x_ref, w_ref, cw_ref, cb_ref, o_ref, wb_ref, *, seq, n_short_tiles):
    i = pl.program_id(1)

    @pl.when(i == 0)
    def _():
        _cast_weight(w_ref, wb_ref)

    tm, tn = o_ref.shape
    x = x_ref[...]
    keep_halo = (i >= n_short_tiles).astype(F32)
    nseg = tm // seq
    win_rows = seq + 16
    zero8 = jnp.zeros((8, CONV_COL_CHUNK), F32)
    for q in range(tn // CONV_COL_CHUNK):
        cols = slice(q * CONV_COL_CHUNK, (q + 1) * CONV_COL_CHUNK)
        raw = jnp.dot(x, wb_ref[:, cols], preferred_element_type=F32)
        wq = cw_ref[:, cols]
        bq = cb_ref[:, cols]
        for s in range(nseg):
            seg = raw[s * seq:(s + 1) * seq]
            top = raw[s * seq - 8:s * seq] * keep_halo if s > 0 else zero8
            bot = raw[(s + 1) * seq:(s + 1) * seq + 8] * keep_halo if s < nseg - 1 else zero8
            win = jnp.concatenate([top, seg, bot], axis=0)
            acc = bq + wq[D_CONV // 2:D_CONV // 2 + 1, :] * seg
            for k in range(D_CONV):
                shift = k - D_CONV // 2
                if shift != 0:
                    moved = pltpu.roll(win, (-shift) % win_rows, 0)
                    acc = acc + wq[k:k + 1, :] * moved[8:8 + seq]
            o_ref[s * seq:(s + 1) * seq, cols] = _silu(acc)


def _mm_conv(x, w, widx, conv_w, conv_b, layer, *, tm, tn, col0, n_out, seq, n_short_tiles):
    t, k = x.shape
    return pl.pallas_call(
        functools.partial(_mm_conv_body, seq=seq, n_short_tiles=n_short_tiles),
        grid=(n_out // tn, t // tm),
        in_specs=[
            pl.BlockSpec((tm, k), lambda j, i: (i, 0)),
            _weight_spec(w, widx, k, tn, col0),
            pl.BlockSpec((None, D_CONV, tn), lambda j, i: (layer, 0, j)),
            pl.BlockSpec((None, 1, tn), lambda j, i: (layer, 0, j)),
        ],
        out_specs=pl.BlockSpec((tm, tn), lambda j, i: (i, j)),
        out_shape=jax.ShapeDtypeStruct((t, n_out), F32),
        scratch_shapes=[pltpu.VMEM((k, tn), BF16)],
        compiler_params=_params(("arbitrary", "arbitrary")),
        name="matmul_conv_silu",
    )(x, w, conv_w, conv_b)


def _ssd_prep_body(dt_ref, dtb_ref, alog_ref, cum_ref, dtt_ref, *, nc):
    a_coef = -jnp.exp(alog_ref[...])
    q = lax.broadcasted_iota(jnp.int32, (SSD_CHUNK, SSD_CHUNK), 0)
    s = lax.broadcasted_iota(jnp.int32, (SSD_CHUNK, SSD_CHUNK), 1)
    lower = (q >= s).astype(F32)
    upper = (q <= s).astype(F32)
    n_fwd = dt_ref.shape[1] // 2
    for c in range(nc):
        x = dt_ref[c * SSD_CHUNK:(c + 1) * SSD_CHUNK, :] + dtb_ref[...]
        dt = jnp.maximum(x, 0.0) + jnp.log1p(jnp.exp(-jnp.abs(x)))
        a = dt * a_coef
        prefix = jnp.dot(lower, a, preferred_element_type=F32, precision=lax.Precision.HIGHEST)
        suffix = jnp.dot(upper, a, preferred_element_type=F32, precision=lax.Precision.HIGHEST)
        cum = jnp.where(s < n_fwd, prefix, suffix)
        cum_ref[c] = cum.T
        dtt_ref[c] = dt.T


def _ssd_prep(dt_raw, dt_bias, a_log, *, row_block0, nseq, l):
    nc = l // SSD_CHUNK
    h2 = dt_bias.size
    out = jax.ShapeDtypeStruct((nseq, nc, h2, SSD_CHUNK), F32)
    out_spec = pl.BlockSpec((None, nc, h2, SSD_CHUNK), lambda b: (b, 0, 0, 0))
    return pl.pallas_call(
        functools.partial(_ssd_prep_body, nc=nc),
        grid=(nseq,),
        in_specs=[
            pl.BlockSpec((l, h2), lambda b: (row_block0 + b, 0)),
            pl.BlockSpec((1, h2), lambda b: (0, 0)),
            pl.BlockSpec((1, h2), lambda b: (0, 0)),
        ],
        out_specs=[out_spec, out_spec],
        out_shape=[out, out],
        compiler_params=_params(("arbitrary",)),
        name="ssd_decay_prep",
    )(dt_raw, dt_bias.reshape(1, h2), a_log.reshape(1, h2))


def _ssd_main_body(*refs, l, seqs_per_step, has_init, emit_state, n_aliased):
    it = iter(refs)
    z_ref, xs_ref, bs_ref, cs_ref, cum_ref, dtt_ref = [next(it) for _ in range(6)]
    dvec_ref, nw_ref, rep_ref = next(it), next(it), next(it)
    s0_ref = next(it) if has_init else None
    for _ in range(n_aliased):
        next(it)
    u_ref, ss_ref = next(it), next(it)
    st_ref = next(it) if emit_state else None
    sloc_ref, sprev_ref, run_ref, yoff_ref = [next(it) for _ in range(4)]

    nc = l // SSD_CHUNK
    g = pl.program_id(1)
    hp = SSD_HEADDIM
    gw = xs_ref.shape[1]
    heads = gw // hp
    pw = 2 * hp
    npairs = heads // 2
    n_fwd = cum_ref.shape[1] // 2
    g8 = pl.multiple_of(g * heads, heads)
    contract_last = (((1,), (1,)), ((), ()))

    qi = lax.broadcasted_iota(jnp.int32, (SSD_CHUNK, SSD_CHUNK), 0)
    si = lax.broadcasted_iota(jnp.int32, (SSD_CHUNK, SSD_CHUNK), 1)
    lower, strict_lower, diag = qi >= si, qi > si, qi == si
    left = si < hp
    lane_head = lax.broadcasted_iota(jnp.int32, (1, gw), 1) // hp

    def chunk_rows(c):
        return pl.ds(pl.multiple_of(c * SSD_CHUNK, SSD_CHUNK), SSD_CHUNK)

    def decay_rows(c):
        cf = cum_ref[c, pl.ds(g8, heads), :]
        cb = cum_ref[c, pl.ds(n_fwd + g8, heads), :]
        return cf, cb, dtt_ref[c, pl.ds(g8, heads), :], dtt_ref[c, pl.ds(n_fwd + g8, heads), :]

    def blockdiag(xp):
        zero = jnp.zeros_like(xp)
        return jnp.concatenate([jnp.where(left, xp, zero).astype(BF16), jnp.where(left, zero, xp).astype(BF16)], axis=0)

    def per_head_lanes(col):
        out = jnp.zeros((1, gw), F32)
        for e in range(heads):
            out = jnp.where(lane_head == e, col[e:e + 1, :], out)
        return out

    def init_states(sq, run):
        if has_init:
            for d in range(2):
                for j in range(npairs):
                    pair = jnp.concatenate([s0_ref[sq, d, 2 * j], s0_ref[sq, d, 2 * j + 1]], axis=0)
                    run[d, :, j * pw:(j + 1) * pw] = pair.T
        else:
            run[...] = jnp.zeros(run.shape, F32)

    def local_states(c, run):
        rows = chunk_rows(c)
        bt = bs_ref[rows, :].T
        cf, cb, dtf, dtb = decay_rows(c)
        tot_f = cf[:, SSD_CHUNK - 1:SSD_CHUNK]
        wf = jnp.exp(tot_f - cf) * dtf
        wb = jnp.exp(cb[:, 0:1] - cb) * dtb
        dec_f = per_head_lanes(jnp.exp(tot_f))
        for j in range(npairs):
            cols_j = slice(j * pw, (j + 1) * pw)
            xbd = blockdiag(xs_ref[rows, cols_j])

            def contribution(w):
                lhs = jnp.concatenate([bt * w[2 * j:2 * j + 1, :], bt * w[2 * j + 1:2 * j + 2, :]], axis=1)
                return jnp.dot(lhs.astype(BF16), xbd, preferred_element_type=F32)

            sf = run[0, :, cols_j]
            sprev_ref[c, :, cols_j] = sf.astype(BF16)
            run[0, :, cols_j] = sf * dec_f[:, cols_j] + contribution(wf)
            sloc_ref[c, :, cols_j] = contribution(wb)

    def out_chunk(c, slot, run):
        rows = chunk_rows(c)
        cc = cs_ref[rows, :].astype(BF16)
        bc = bs_ref[rows, :].astype(BF16)
        cbm = lax.dot_general(cc, bc, contract_last, preferred_element_type=F32)
        cf, cb, dtf, dtb = decay_rows(c)
        ex = jnp.exp(jnp.concatenate([cf, cb], axis=0))
        pieces, rem = [], ex
        for _ in range(N_SPLIT):
            p = rem.astype(BF16).astype(F32)
            pieces.append(p)
            rem = rem - p
        filler = jnp.zeros((SSD_CHUNK - 2 * heads * (1 + N_SPLIT), SSD_CHUNK), F32)
        cols = jnp.concatenate([cf, cb] + pieces + [filler], axis=0).T
        ecols = jnp.where(si >= 2 * heads, cols, 0.0).astype(BF16)
        e_all = jnp.dot(ecols, rep_ref[...], preferred_element_type=F32)
        sb = run[1]
        yoff_ref[slot] = (e_all[:, 0:gw] * jnp.dot(cc, sprev_ref[c], preferred_element_type=F32)
                          + e_all[:, gw:2 * gw] * jnp.dot(cc, sb.astype(BF16), preferred_element_type=F32))
        run[1] = sb * per_head_lanes(jnp.exp(cb[:, 0:1])) + sloc_ref[c]
        part = jnp.zeros((SSD_CHUNK, pw), F32)
        for j in range(npairs):
            cols_j = slice(j * pw, (j + 1) * pw)
            xj = xs_ref[rows, cols_j]
            ms = []
            for e in (2 * j, 2 * j + 1):
                arg = jnp.where(lower, cols[:, e:e + 1] - cf[e:e + 1, :],
                                cols[:, heads + e:heads + e + 1] - cb[e:e + 1, :])
                dtf_e, dtb_e = dtf[e:e + 1, :], dtb[e:e + 1, :]
                coef = jnp.where(strict_lower, dtf_e, jnp.where(diag, dtf_e + dtb_e, dtb_e))
                ms.append(cbm * (jnp.exp(arg) * coef))
            lhs = jnp.concatenate(ms, axis=1).astype(BF16)
            yj = (dvec_ref[:, cols_j] * xj + yoff_ref[slot, :, cols_j]
                  + jnp.dot(lhs, blockdiag(xj), preferred_element_type=F32))
            v = yj * _silu(z_ref[rows, cols_j])
            part = part + v * v
            u_ref[rows, cols_j] = (v * nw_ref[:, cols_j]).astype(BF16)

        @pl.when(g == 0)
        def _():
            ss_ref[rows, :] = part

        @pl.when(g != 0)
        def _():
            ss_ref[rows, :] = ss_ref[rows, :] + part

    def emit_states(sq, run):
        for d in range(2):
            for j in range(npairs):
                pair = run[d, :, j * pw:(j + 1) * pw].T
                st_ref[sq, d, 2 * j] = pair[0:hp]
                st_ref[sq, d, 2 * j + 1] = pair[hp:2 * hp]

    for sq in range(seqs_per_step):
        base = sq * nc
        run = run_ref.at[sq % 2]
        init_states(sq, run)

        def ascending(ii, carry, base=base, run=run):
            for k in range(2):
                local_states(base + 2 * ii + k, run)
            return carry

        def descending(ii, carry, base=base, run=run):
            for k in range(2):
                out_chunk(base + nc - 1 - (2 * ii + k), k, run)
            return carry

        lax.fori_loop(0, nc // 2, ascending, 0)
        lax.fori_loop(0, nc // 2, descending, 0)
        if emit_state:
            emit_states(sq, run)


def _ssd_main(z, xbc, cum, dtt, dvec, norm_w, rep, s0, bufs, state_out, *, row_block0, nseq, l, seqs_per_step, layer,
              d_inner):
    t = z.shape[0]
    heads = d_inner // SSD_HEADDIM // SSD_GROUPS
    gw = heads * SSD_HEADDIM
    n = SSD_STATE
    bb = d_inner // n
    cb = bb + SSD_GROUPS
    nc = l // SSD_CHUNK
    sps = seqs_per_step
    rows = sps * l
    has_init = s0 is not None
    emit_state = state_out is not None
    assert nseq % sps == 0 and nc % 2 == 0
    h2 = cum.shape[2]
    cum = cum.reshape(nseq // sps, sps * nc, h2, SSD_CHUNK)
    dtt = dtt.reshape(nseq // sps, sps * nc, h2, SSD_CHUNK)

    def rb(b):
        return row_block0 + b

    def lay(width, off):
        return pl.BlockSpec((None, 1, width), lambda b, g: (layer, 0, off + g))

    state_spec = pl.BlockSpec((sps, None, 2, heads, SSD_HEADDIM, n), lambda b, g: (b, layer, 0, g, 0, 0))
    in_specs = [
        pl.BlockSpec((rows, gw), lambda b, g: (rb(b), g)),
        pl.BlockSpec((rows, gw), lambda b, g: (rb(b), g)),
        pl.BlockSpec((rows, n), lambda b, g: (rb(b), bb + g)),
        pl.BlockSpec((rows, n), lambda b, g: (rb(b), cb + g)),
        pl.BlockSpec((None, sps * nc, h2, SSD_CHUNK), lambda b, g: (b, 0, 0, 0)),
        pl.BlockSpec((None, sps * nc, h2, SSD_CHUNK), lambda b, g: (b, 0, 0, 0)),
        lay(gw, 0),
        lay(gw, 0),
        pl.BlockSpec(rep.shape, lambda b, g: (0, 0)),
    ]
    args = [z, xbc, xbc, xbc, cum, dtt, dvec, norm_w, rep]
    if has_init:
        in_specs.append(state_spec)
        args.append(s0)
    aliases = {}
    out_shape = [jax.ShapeDtypeStruct((t, d_inner), BF16), jax.ShapeDtypeStruct((t, 128), F32)]
    out_specs = [pl.BlockSpec((rows, gw), lambda b, g: (rb(b), g)), pl.BlockSpec((rows, 128), lambda b, g: (rb(b), 0))]
    any_spec = pl.BlockSpec(memory_space=pl.ANY)

    def alias(buf, out_idx):
        aliases[len(args)] = out_idx
        in_specs.append(any_spec)
        args.append(buf)

    if bufs is not None:
        alias(bufs[0], 0)
        alias(bufs[1], 1)
    if emit_state:
        if not isinstance(state_out, jax.ShapeDtypeStruct):
            alias(state_out, 2)
        out_shape.append(jax.ShapeDtypeStruct(state_out.shape, F32))
        out_specs.append(state_spec)
    return pl.pallas_call(
        functools.partial(_ssd_main_body, l=l, seqs_per_step=sps, has_init=has_init, emit_state=emit_state,
                          n_aliased=len(aliases)),
        grid=(nseq // sps, SSD_GROUPS),
        in_specs=in_specs,
        out_specs=out_specs,
        out_shape=out_shape,
        input_output_aliases=aliases,
        scratch_shapes=[
            pltpu.VMEM((sps * nc, n, gw), F32),
            pltpu.VMEM((sps * nc, n, gw), BF16),
            pltpu.VMEM((2, 2, n, gw), F32),
            pltpu.VMEM((2, SSD_CHUNK, gw), F32),
        ],
        compiler_params=_params(("arbitrary", "arbitrary")),
        name="ssd_scan_gate",
    )(*args)


def _head_lane_spreader(heads, gw):
    hd2 = 2 * heads
    rows = jnp.arange(SSD_CHUNK)[:, None]
    lanes = jnp.arange(2 * gw)[None, :]
    hit = (rows >= hd2) & (rows < hd2 * (1 + N_SPLIT)) & ((rows % hd2) == lanes // SSD_HEADDIM)
    return hit.astype(BF16)


def _attn_body(*refs, l, tq, has_cache, rope, emit_kv, scale, n_aliased):
    it = iter(refs)
    q_ref, k_ref, v_ref, qn_ref, kn_ref = [next(it) for _ in range(5)]
    cos_ref, sin_ref = (next(it), next(it)) if rope else (None, None)
    ck_ref, cv_ref = (next(it), next(it)) if has_cache else (None, None)
    for _ in range(n_aliased):
        next(it)
    o_ref = next(it)
    ko_ref, vo_ref = (next(it), next(it)) if emit_kv else (None, None)
    kb_ref = next(it)

    lane = lax.broadcasted_iota(jnp.int32, (1, HEAD_DIM), 1)
    even = (lane % 2) == 0
    contract_last = (((1,), (1,)), ((), ()))

    def norm_rope(x, w, rows):
        r = lax.rsqrt(jnp.mean(x * x, axis=-1, keepdims=True) + EPS)
        xn = (x * r) * w
        if rope:
            partner = jnp.where(even, pltpu.roll(xn, HEAD_DIM - 1, 1), pltpu.roll(xn, 1, 1))
            xn = xn * cos_ref[rows, :] + partner * sin_ref[rows, :]
        return xn

    def kv_head(kv):
        kcols = slice(kv * HEAD_DIM, (kv + 1) * HEAD_DIM)
        for c in range(l // tq):
            rows = slice(c * tq, (c + 1) * tq)
            kn = norm_rope(k_ref[rows, kcols], kn_ref[...], rows)
            kb_ref[rows, kcols] = kn.astype(BF16)
            if emit_kv:
                ko_ref[rows, kcols] = kn
        v = v_ref[:, kcols]
        if emit_kv:
            vo_ref[:, kcols] = v
        vb = v.astype(BF16)
        kb = kb_ref[:, kcols]
        if has_cache:
            ckb = ck_ref[:, kcols].astype(BF16)
            cvb = cv_ref[:, kcols].astype(BF16)

        for c in range(l // tq):
            rows = slice(c * tq, (c + 1) * tq)
            for hh in range(GQA_GROUP):
                head = kv * GQA_GROUP + hh
                cols = slice(head * HEAD_DIM, (head + 1) * HEAD_DIM)
                qn = norm_rope(q_ref[rows, cols], qn_ref[...], rows).astype(BF16)
                s = lax.dot_general(qn, kb, contract_last, preferred_element_type=F32) * scale
                mx = jnp.max(s, axis=-1, keepdims=True)
                if has_cache:
                    sc = lax.dot_general(qn, ckb, contract_last, preferred_element_type=F32) * scale
                    mx = jnp.maximum(mx, jnp.max(sc, axis=-1, keepdims=True))
                p = jnp.exp(s - mx)
                den = jnp.sum(p, axis=-1, keepdims=True)
                o = jnp.dot(p.astype(BF16), vb, preferred_element_type=F32)
                if has_cache:
                    pc = jnp.exp(sc - mx)
                    den = den + jnp.sum(pc, axis=-1, keepdims=True)
                    o = o + jnp.dot(pc.astype(BF16), cvb, preferred_element_type=F32)
                o_ref[rows, cols] = (o / den).astype(o_ref.dtype)

    for kv in range(k_ref.shape[1] // HEAD_DIM):
        kv_head(kv)


def _attention(qkv, q_norm, k_norm, tables, caches, o_buf, kv_out, *, row_block0, nseq, l, layer, n_heads, n_kv,
               kv_per_step):
    t = qkv.shape[0]
    kw = kv_per_step * HEAD_DIM
    gq = GQA_GROUP * kw
    assert n_kv % kv_per_step == 0 and n_heads % kv_per_step == 0
    kcol = n_heads // kv_per_step
    vcol = (n_heads + n_kv) // kv_per_step
    tq = min(l, 256)

    def rb(b):
        return row_block0 + b

    norm_spec = pl.BlockSpec((None, 1, HEAD_DIM), lambda b, j: (layer, 0, 0))
    in_specs = [
        pl.BlockSpec((l, gq), lambda b, j: (rb(b), j)),
        pl.BlockSpec((l, kw), lambda b, j: (rb(b), kcol + j)),
        pl.BlockSpec((l, kw), lambda b, j: (rb(b), vcol + j)),
        norm_spec, norm_spec,
    ]
    args = [qkv, qkv, qkv, q_norm, k_norm]
    if tables is not None:
        in_specs += [pl.BlockSpec((l, HEAD_DIM), lambda b, j: (0, 0))] * 2
        args += list(tables)
    if caches is not None:
        past = caches[0].shape[2]
        in_specs += [pl.BlockSpec((None, None, past, kw), lambda b, j: (b, layer, 0, j))] * 2
        args += list(caches)
    any_spec = pl.BlockSpec(memory_space=pl.ANY)
    aliases = {}

    def alias(buf, out_idx):
        aliases[len(args)] = out_idx
        in_specs.append(any_spec)
        args.append(buf)

    if o_buf is not None:
        alias(o_buf, 0)
    out_shape = [jax.ShapeDtypeStruct((t, n_heads * HEAD_DIM), BF16)]
    out_specs = [pl.BlockSpec((l, gq), lambda b, j: (rb(b), j))]
    if kv_out is not None:
        fresh = isinstance(kv_out, jax.ShapeDtypeStruct)
        for k in range(2):
            if not fresh:
                alias(kv_out[k], 1 + k)
            out_shape.append(jax.ShapeDtypeStruct(kv_out.shape if fresh else kv_out[k].shape, F32))
            out_specs.append(pl.BlockSpec((None, None, l, kw), lambda b, j: (b, layer, 0, j)))
    return pl.pallas_call(
        functools.partial(_attn_body, l=l, tq=tq, has_cache=caches is not None, rope=tables is not None,
                          emit_kv=kv_out is not None, scale=HEAD_DIM ** -0.5, n_aliased=len(aliases)),
        grid=(nseq, n_kv // kv_per_step),
        in_specs=in_specs,
        out_specs=out_specs,
        out_shape=out_shape,
        input_output_aliases=aliases,
        scratch_shapes=[pltpu.VMEM((l, kw), BF16)],
        compiler_params=_params(("arbitrary", "arbitrary")),
        name="gqa_attention",
    )(*args)


def _rope_tables(n_tok):
    pairs = HEAD_DIM // 4
    rows = n_tok // GRID_W
    row_id = jnp.broadcast_to(jnp.arange(rows, dtype=F32)[:, None], (rows, GRID_W)).reshape(-1)
    col_id = jnp.broadcast_to(jnp.arange(GRID_W, dtype=F32)[None, :], (rows, GRID_W)).reshape(-1)
    inv = ROPE_THETA ** (-jnp.arange(pairs, dtype=F32) / pairs)
    ang = jnp.concatenate([row_id[:, None] * inv, col_id[:, None] * inv], axis=-1)
    cos, sin = jnp.cos(ang), jnp.sin(ang)
    cos_full = jnp.repeat(cos, 2, axis=-1)
    sin_signed = jnp.stack([-sin, sin], axis=-1).reshape(n_tok, HEAD_DIM)
    return cos_full, sin_signed


def kernel(x_prompt, x_sample, c, state_ssd, cache_k, cache_v, c_ctx, ada_w, ada_b, norm_pre, norm_post, ffn_w1, ffn_w3, ffn_w2, ssd_in_w, ssd_conv_w, ssd_conv_b, ssd_dt_bias, ssd_a_log, ssd_d, ssd_norm_w, ssd_out_w, attn_qkv_w, attn_q_norm, attn_k_norm, attn_o_w):
    batch, seq, d = x_prompt.shape
    dec_batch, dec_seq, _ = x_sample.shape
    depth = ada_w.shape[0]
    n_ctx = batch * seq
    n_lat = dec_batch * dec_seq
    t = n_ctx + n_lat
    d_inner = ssd_out_w.shape[1]
    n_ssd_heads = ssd_d.shape[1]
    n_kv = cache_k.shape[3]
    n_heads = attn_o_w.shape[1] // HEAD_DIM
    n_ssd_layers = ssd_in_w.shape[0]
    n_attn_layers = attn_qkv_w.shape[0]
    past = cache_k.shape[2]
    assert dec_seq % ROW_TILE == 0 and n_ctx % ROW_TILE == 0 and dec_batch + 1 <= N_COND_ROWS
    ctx_rb0, lat_rb0 = 0, n_ctx // dec_seq

    cond = jnp.concatenate([c_ctx[None, :], c, jnp.zeros((N_COND_ROWS - 1 - dec_batch, d), F32)], axis=0)
    modr = _modulation(cond, ada_w, ada_b).reshape(depth * N_COND_ROWS * N_MOD, 1, d)
    norm_pre_r = norm_pre.reshape(depth, 3, 1, d)
    norm_post_r = norm_post.reshape(depth, 3, 1, d)

    def postpre(xs, y, post, pre):
        post_a = pre_a = None
        if post is not None:
            l, j, res_w = post
            post_a = (norm_post_r[l, j], (l * N_COND_ROWS) * N_MOD + 3 * j, res_w)
        if pre is not None:
            l, j = pre
            pre_a = (norm_pre_r[l, j], (l * N_COND_ROWS) * N_MOD + 3 * j)
        out_c = _postpre(xs[0], y, None, modr, t, row0=0, cond_row0=0, rows_per_cond=n_ctx, post=post_a, pre=pre_a)
        h = out_c[-1] if pre is not None else None
        out_l = _postpre(xs[1], y, h, modr, t, row0=n_ctx, cond_row0=1, rows_per_cond=dec_seq, post=post_a, pre=pre_a)
        new_xs = (out_c[0], out_l[0]) if post is not None else xs
        return new_xs, (out_l[-1] if pre is not None else None)

    def ffn(h, l, k):
        a = _gateup(h, ffn_w1, ffn_w3, (l, k), tm=ROW_TILE, tn=512)
        return _mm(a, ffn_w2, (l, k), tm=512, tn=512, out_dtype=F32)

    cos_sin = _rope_tables(dec_seq)
    cache_k2 = cache_k.reshape(dec_batch, n_attn_layers, past, n_kv * HEAD_DIM)
    cache_v2 = cache_v.reshape(dec_batch, n_attn_layers, past, n_kv * HEAD_DIM)
    q_norm_r = attn_q_norm.reshape(n_attn_layers, 1, HEAD_DIM)
    k_norm_r = attn_k_norm.reshape(n_attn_layers, 1, HEAD_DIM)
    new_kv = jax.ShapeDtypeStruct((batch, n_attn_layers, seq, n_kv * HEAD_DIM), F32)
    new_state = jax.ShapeDtypeStruct((batch, n_ssd_layers, 2, n_ssd_heads, SSD_HEADDIM, SSD_STATE), F32)
    conv_dim = ssd_conv_w.shape[-1]
    dt_width = 2 * n_ssd_heads
    assert seq % SSD_CHUNK == 0 and ROW_TILE % seq == 0 and dec_seq == ROW_TILE
    assert d_inner % 1024 == 0 and conv_dim % 1024 == 0 and (d_inner + conv_dim) % dt_width == 0
    conv_b_r = ssd_conv_b.reshape(n_ssd_layers, 1, conv_dim)
    dvec_r = jnp.repeat(ssd_d, SSD_HEADDIM, axis=-1).reshape(n_ssd_layers, 1, d_inner)
    ssd_nw_r = ssd_norm_w.reshape(n_ssd_layers, 1, d_inner)
    group_heads = n_ssd_heads // SSD_GROUPS
    rep = _head_lane_spreader(group_heads, group_heads * SSD_HEADDIM)
    n_mix = depth // n_attn_layers

    xs = (x_prompt.reshape(n_ctx, d), x_sample.reshape(n_lat, d))
    _, h = postpre(xs, None, None, (0, 0))
    for l in range(depth):
        y = ffn(h, l, 0)
        xs, h = postpre(xs, y, (l, 0, 0.5), (l, 1))
        i = l // n_mix
        if l % n_mix == 0:
            z = _mm(h, ssd_in_w, (i,), tm=ROW_TILE, tn=1024, out_dtype=F32, n_out=d_inner)
            xbc = _mm_conv(h, ssd_in_w, (i,), ssd_conv_w, conv_b_r, i, tm=ROW_TILE, tn=1024, col0=d_inner // 1024,
                           n_out=conv_dim, seq=seq, n_short_tiles=n_ctx // ROW_TILE)
            dt_raw = _mm(h, ssd_in_w, (i,), tm=ROW_TILE, tn=dt_width, out_dtype=F32,
                         col0=(d_inner + conv_dim) // dt_width, n_out=dt_width)
            cum, dtt = _ssd_prep(dt_raw, ssd_dt_bias[i], ssd_a_log[i], row_block0=ctx_rb0, nseq=batch, l=seq)
            u, ss, new_state = _ssd_main(z, xbc, cum, dtt, dvec_r, ssd_nw_r, rep, None, None, new_state,
                                         row_block0=ctx_rb0, nseq=batch, l=seq, seqs_per_step=ROW_TILE // seq,
                                         layer=i, d_inner=d_inner)
            cum, dtt = _ssd_prep(dt_raw, ssd_dt_bias[i], ssd_a_log[i], row_block0=lat_rb0, nseq=dec_batch, l=dec_seq)
            u, ss = _ssd_main(z, xbc, cum, dtt, dvec_r, ssd_nw_r, rep, state_ssd, (u, ss), None,
                              row_block0=lat_rb0, nseq=dec_batch, l=dec_seq, seqs_per_step=1, layer=i,
                              d_inner=d_inner)
            y = _mm(u, ssd_out_w, (i,), tm=ROW_TILE, tn=512, out_dtype=F32, ss=ss, inv_width=1.0 / d_inner)
        else:
            qkv = _mm(h, attn_qkv_w, (i,), tm=ROW_TILE, tn=1024, out_dtype=F32)
            o, *new_kv = _attention(qkv, q_norm_r, k_norm_r, None, None, None, new_kv,
                                    row_block0=ctx_rb0, nseq=batch, l=seq, layer=i, n_heads=n_heads, n_kv=n_kv,
                                    kv_per_step=n_kv)
            (o,) = _attention(qkv, q_norm_r, k_norm_r, cos_sin, (cache_k2, cache_v2), o, None,
                              row_block0=lat_rb0, nseq=dec_batch, l=dec_seq, layer=i, n_heads=n_heads, n_kv=n_kv,
                              kv_per_step=1)
            y = _mm(o, attn_o_w, (i,), tm=ROW_TILE, tn=1024, out_dtype=F32)
        xs, h = postpre(xs, y, (l, 1, 1.0), (l, 2))
        y = ffn(h, l, 1)
        xs, h = postpre(xs, y, (l, 2, 0.5), (l + 1, 0) if l + 1 < depth else None)

    y_prompt = xs[0].reshape(batch, seq, d)
    y_sample = xs[1].reshape(dec_batch, dec_seq, d)
    new_cache_k = new_kv[0].reshape(batch, n_attn_layers, seq, n_kv, HEAD_DIM)
    new_cache_v = new_kv[1].reshape(batch, n_attn_layers, seq, n_kv, HEAD_DIM)
    return (y_prompt, y_sample, new_state.astype(x_prompt.dtype), new_cache_k, new_cache_v)
```

```python
import functools

import jax
import jax.numpy as jnp
from jax import lax
from jax.experimental import pallas as pl
from jax.experimental.pallas import tpu as pltpu

F32 = jnp.float32
BF16 = jnp.bfloat16

N_MOD = 9
N_COND_ROWS = 8
SSD_HEADDIM = 64
SSD_GROUPS = 8
SSD_STATE = 128
SSD_CHUNK = 128
D_CONV = 5
HEAD_DIM = 128
GQA_GROUP = 4
GRID_W = 64
ROPE_THETA = 10000.0
EPS = 1e-6

V7X_VMEM_LIMIT_BYTES = 56 * 1024 * 1024
ROW_TILE = 1024
NORM_ROW_TILE = 512
W_CAST_ROWS = 256
CONV_COL_CHUNK = 256
N_SPLIT = 3


def _params(sem):
    return pltpu.CompilerParams(dimension_semantics=sem, vmem_limit_bytes=V7X_VMEM_LIMIT_BYTES)


def _silu(x):
    h = 0.5 * x
    return h + h * jnp.tanh(h)


def _cast_weight(w_ref, wb_ref):
    k = w_ref.shape[0]
    rows = W_CAST_ROWS if k % W_CAST_ROWS == 0 else k

    def body(r, carry):
        sl = pl.ds(pl.multiple_of(r * rows, rows), rows)
        wb_ref[sl, :] = w_ref[sl, :].astype(BF16)
        return carry

    lax.fori_loop(0, k // rows, body, 0)


def _weight_spec(w, widx, k, tn, col0=0):
    lead = tuple(widx)
    return pl.BlockSpec((None,) * len(lead) + (k, tn), lambda j, i: lead + (0, col0 + j))


def _mod_body(c_ref, w_ref, b_ref, o_ref):
    s = _silu(c_ref[...]).astype(BF16)
    o_ref[...] = jnp.dot(s, w_ref[...].astype(BF16), preferred_element_type=F32) + b_ref[...]


def _modulation(cond, ada_w, ada_b):
    depth, d, n = ada_w.shape
    tn = 1024
    return pl.pallas_call(
        _mod_body,
        grid=(depth, n // tn),
        in_specs=[
            pl.BlockSpec((N_COND_ROWS, d), lambda l, j: (0, 0)),
            pl.BlockSpec((None, d, tn), lambda l, j: (l, 0, j)),
            pl.BlockSpec((None, 1, tn), lambda l, j: (l, 0, j)),
        ],
        out_specs=pl.BlockSpec((None, N_COND_ROWS, tn), lambda l, j: (l, 0, j)),
        out_shape=jax.ShapeDtypeStruct((depth, N_COND_ROWS, n), F32),
        compiler_params=_params(("arbitrary", "arbitrary")),
        name="adaln_modulation",
    )(cond, ada_w, ada_b.reshape(depth, 1, n))


def _postpre_body(*refs, has_post, has_pre, res_w, n_first):
    it = iter(refs)
    xa_ref, xb_ref = next(it), next(it)
    if has_post:
        y_ref, gpost_ref, gate_ref = next(it), next(it), next(it)
    if has_pre:
        gpre_ref, scale_ref, shift_ref = next(it), next(it), next(it)
    if has_post:
        xao_ref, xbo_ref = next(it), next(it)
    if has_pre:
        h_ref = next(it)

    def step(x_ref, xo_ref):
        x = x_ref[...]
        if has_post:
            y = y_ref[...]
            r = lax.rsqrt(jnp.mean(y * y, axis=-1, keepdims=True) + EPS)
            x = x + (res_w * gate_ref[...]) * ((y * r) * gpost_ref[...])
            xo_ref[...] = x
        if has_pre:
            r = lax.rsqrt(jnp.mean(x * x, axis=-1, keepdims=True) + EPS)
            h = ((x * r) * gpre_ref[...]) * (1.0 + scale_ref[...]) + shift_ref[...]
            h_ref[...] = h.astype(BF16)

    first = pl.program_id(0) < n_first

    @pl.when(first)
    def _():
        step(xa_ref, xao_ref if has_post else None)

    @pl.when(jnp.logical_not(first))
    def _():
        step(xb_ref, xbo_ref if has_post else None)


def _postpre(xa, xb, y, modr, *, rows_per_cond_b, post, pre):
    (ra, d), rb = xa.shape, xb.shape[0]
    tm = NORM_ROW_TILE
    na, nb = ra // tm, rb // tm

    def cond_row(i):
        return jnp.where(i < na, 0, 1 + (jnp.maximum(i - na, 0) * tm) // rows_per_cond_b)

    def mod_spec(base, k):
        return pl.BlockSpec((None, 1, d), lambda i: (base + cond_row(i) * N_MOD + k, 0, 0))

    a_spec = pl.BlockSpec((tm, d), lambda i: (jnp.minimum(i, na - 1), 0))
    b_spec = pl.BlockSpec((tm, d), lambda i: (jnp.maximum(i - na, 0), 0))
    tok_spec = pl.BlockSpec((tm, d), lambda i: (i, 0))
    vec_spec = pl.BlockSpec((1, d), lambda i: (0, 0))
    args, in_specs, out_shape, out_specs = [xa, xb], [a_spec, b_spec], [], []
    res_w = 0.0
    if post is not None:
        g_post, base, res_w = post
        args += [y, g_post, modr]
        in_specs += [tok_spec, vec_spec, mod_spec(base, 2)]
        out_shape += [jax.ShapeDtypeStruct((ra, d), F32), jax.ShapeDtypeStruct((rb, d), F32)]
        out_specs += [a_spec, b_spec]
    if pre is not None:
        g_pre, base = pre
        args += [g_pre, modr, modr]
        in_specs += [vec_spec, mod_spec(base, 1), mod_spec(base, 0)]
        out_shape.append(jax.ShapeDtypeStruct((ra + rb, d), BF16))
        out_specs.append(tok_spec)
    return pl.pallas_call(
        functools.partial(_postpre_body, has_post=post is not None, has_pre=pre is not None, res_w=res_w, n_first=na),
        grid=(na + nb,),
        in_specs=in_specs,
        out_specs=out_specs,
        out_shape=out_shape,
        compiler_params=_params(("arbitrary",)),
        name="norm_modulate_residual",
    )(*args)


def _mm_body(*refs, has_scale, inv_width):
    if has_scale:
        x_ref, w_ref, ss_ref, o_ref, wb_ref = refs
    else:
        x_ref, w_ref, o_ref, wb_ref = refs

    @pl.when(pl.program_id(1) == 0)
    def _():
        _cast_weight(w_ref, wb_ref)

    acc = jnp.dot(x_ref[...], wb_ref[...], preferred_element_type=F32)
    if has_scale:
        ms = jnp.sum(ss_ref[...], axis=-1, keepdims=True) * inv_width
        acc = acc * lax.rsqrt(ms + EPS)
    o_ref[...] = acc.astype(o_ref.dtype)


def _mm(x, w, widx, *, tm, tn, out_dtype, ss=None, inv_width=1.0, col0=0, n_out=None):
    t, k = x.shape
    n = w.shape[-1] if n_out is None else n_out
    args = [x, w]
    in_specs = [pl.BlockSpec((tm, k), lambda j, i: (i, 0)), _weight_spec(w, widx, k, tn, col0)]
    if ss is not None:
        args.append(ss)
        in_specs.append(pl.BlockSpec((tm, ss.shape[1]), lambda j, i: (i, 0)))
    return pl.pallas_call(
        functools.partial(_mm_body, has_scale=ss is not None, inv_width=inv_width),
        grid=(n // tn, t // tm),
        in_specs=in_specs,
        out_specs=pl.BlockSpec((tm, tn), lambda j, i: (i, j)),
        out_shape=jax.ShapeDtypeStruct((t, n), out_dtype),
        scratch_shapes=[pltpu.VMEM((k, tn), BF16)],
        compiler_params=_params(("arbitrary", "arbitrary")),
        name="matmul_ws",
    )(*args)


def _gateup_body(x_ref, w1_ref, w3_ref, o_ref, w1b_ref, w3b_ref):
    @pl.when(pl.program_id(1) == 0)
    def _():
        _cast_weight(w1_ref, w1b_ref)
        _cast_weight(w3_ref, w3b_ref)

    x = x_ref[...]
    a = jnp.dot(x, w1b_ref[...], preferred_element_type=F32)
    b = jnp.dot(x, w3b_ref[...], preferred_element_type=F32)
    o_ref[...] = (_silu(a) * b).astype(o_ref.dtype)


def _gateup(x, w1, w3, widx, *, tm, tn):
    t, k = x.shape
    n = w1.shape[-1]
    return pl.pallas_call(
        _gateup_body,
        grid=(n // tn, t // tm),
        in_specs=[pl.BlockSpec((tm, k), lambda j, i: (i, 0)), _weight_spec(w1, widx, k, tn),
                  _weight_spec(w3, widx, k, tn)],
        out_specs=pl.BlockSpec((tm, tn), lambda j, i: (i, j)),
        out_shape=jax.ShapeDtypeStruct((t, n), BF16),
        scratch_shapes=[pltpu.VMEM((k, tn), BF16), pltpu.VMEM((k, tn), BF16)],
        compiler_params=_params(("arbitrary", "arbitrary")),
        name="ffn_gate_up",
    )(x, w1, w3)


def _mm_conv_body(x_ref, w_ref, cw_ref, cb_ref, o_ref, wb_ref, *, seq, n_short_tiles):
    i = pl.program_id(1)

    @pl.when(i == 0)
    def _():
        _cast_weight(w_ref, wb_ref)

    tm, tn = o_ref.shape
    x = x_ref[...]
    keep_halo = (i >= n_short_tiles).astype(F32)
    nseg = tm // seq
    win_rows = seq + 16
    zero8 = jnp.zeros((8, CONV_COL_CHUNK), F32)
    for q in range(tn // CONV_COL_CHUNK):
        cols = slice(q * CONV_COL_CHUNK, (q + 1) * CONV_COL_CHUNK)
        raw = jnp.dot(x, wb_ref[:, cols], preferred_element_type=F32)
        wq = cw_ref[:, cols]
        bq = cb_ref[:, cols]
        for s in range(nseg):
            seg = raw[s * seq:(s + 1) * seq]
            top = raw[s * seq - 8:s * seq] * keep_halo if s > 0 else zero8
            bot = raw[(s + 1) * seq:(s + 1) * seq + 8] * keep_halo if s < nseg - 1 else zero8
            win = jnp.concatenate([top, seg, bot], axis=0)
            acc = bq + wq[D_CONV // 2:D_CONV // 2 + 1, :] * seg
            for k in range(D_CONV):
                shift = k - D_CONV // 2
                if shift != 0:
                    moved = pltpu.roll(win, (-shift) % win_rows, 0)
                    acc = acc + wq[k:k + 1, :] * moved[8:8 + seq]
            o_ref[s * seq:(s + 1) * seq, cols] = _silu(acc)


def _mm_conv(x, w, widx, conv_w, conv_b, layer, *, tm, tn, col0, n_out, seq, n_short_tiles):
    t, k = x.shape
    return pl.pallas_call(
        functools.partial(_mm_conv_body, seq=seq, n_short_tiles=n_short_tiles),
        grid=(n_out // tn, t // tm),
        in_specs=[
            pl.BlockSpec((tm, k), lambda j, i: (i, 0)),
            _weight_spec(w, widx, k, tn, col0),
            pl.BlockSpec((None, D_CONV, tn), lambda j, i: (layer, 0, j)),
            pl.BlockSpec((None, 1, tn), lambda j, i: (layer, 0, j)),
        ],
        out_specs=pl.BlockSpec((tm, tn), lambda j, i: (i, j)),
        out_shape=jax.ShapeDtypeStruct((t, n_out), F32),
        scratch_shapes=[pltpu.VMEM((k, tn), BF16)],
        compiler_params=_params(("arbitrary", "arbitrary")),
        name="matmul_conv_silu",
    )(x, w, conv_w, conv_b)


def _ssd_prep_body(dt_ref, dtb_ref, alog_ref, cum_ref, dtt_ref, *, nc):
    a_coef = -jnp.exp(alog_ref[...])
    q = lax.broadcasted_iota(jnp.int32, (SSD_CHUNK, SSD_CHUNK), 0)
    s = lax.broadcasted_iota(jnp.int32, (SSD_CHUNK, SSD_CHUNK), 1)
    lower = (q >= s).astype(F32)
    upper = (q <= s).astype(F32)
    n_fwd = dt_ref.shape[1] // 2
    for c in range(nc):
        x = dt_ref[c * SSD_CHUNK:(c + 1) * SSD_CHUNK, :] + dtb_ref[...]
        dt = jnp.maximum(x, 0.0) + jnp.log1p(jnp.exp(-jnp.abs(x)))
        a = dt * a_coef
        prefix = jnp.dot(lower, a, preferred_element_type=F32, precision=lax.Precision.HIGHEST)
        suffix = jnp.dot(upper, a, preferred_element_type=F32, precision=lax.Precision.HIGHEST)
        cum = jnp.where(s < n_fwd, prefix, suffix)
        cum_ref[c] = cum.T
        dtt_ref[c] = dt.T


def _ssd_prep(dt_raw, dt_bias, a_log, *, row_block0, nseq, l):
    nc = l // SSD_CHUNK
    h2 = dt_bias.size
    out = jax.ShapeDtypeStruct((nseq, nc, h2, SSD_CHUNK), F32)
    out_spec = pl.BlockSpec((None, nc, h2, SSD_CHUNK), lambda b: (b, 0, 0, 0))
    return pl.pallas_call(
        functools.partial(_ssd_prep_body, nc=nc),
        grid=(nseq,),
        in_specs=[
            pl.BlockSpec((l, h2), lambda b: (row_block0 + b, 0)),
            pl.BlockSpec((1, h2), lambda b: (0, 0)),
            pl.BlockSpec((1, h2), lambda b: (0, 0)),
        ],
        out_specs=[out_spec, out_spec],
        out_shape=[out, out],
        compiler_params=_params(("arbitrary",)),
        name="ssd_decay_prep",
    )(dt_raw, dt_bias.reshape(1, h2), a_log.reshape(1, h2))


def _ssd_main_body(*refs, l, seqs_per_step, has_init, emit_state, n_aliased):
    it = iter(refs)
    z_ref, xs_ref, bs_ref, cs_ref, cum_ref, dtt_ref = [next(it) for _ in range(6)]
    dvec_ref, nw_ref, rep_ref = next(it), next(it), next(it)
    s0_ref = next(it) if has_init else None
    for _ in range(n_aliased):
        next(it)
    u_ref, ss_ref = next(it), next(it)
    st_ref = next(it) if emit_state else None
    sloc_ref, sprev_ref, run_ref, yoff_ref = [next(it) for _ in range(4)]

    nc = l // SSD_CHUNK
    g = pl.program_id(1)
    hp = SSD_HEADDIM
    gw = xs_ref.shape[1]
    heads = gw // hp
    pw = 2 * hp
    npairs = heads // 2
    n_fwd = cum_ref.shape[1] // 2
    g8 = pl.multiple_of(g * heads, heads)
    contract_last = (((1,), (1,)), ((), ()))

    qi = lax.broadcasted_iota(jnp.int32, (SSD_CHUNK, SSD_CHUNK), 0)
    si = lax.broadcasted_iota(jnp.int32, (SSD_CHUNK, SSD_CHUNK), 1)
    lower, strict_lower, diag = qi >= si, qi > si, qi == si
    left = si < hp
    lane_head = lax.broadcasted_iota(jnp.int32, (1, gw), 1) // hp

    def chunk_rows(c):
        return pl.ds(pl.multiple_of(c * SSD_CHUNK, SSD_CHUNK), SSD_CHUNK)

    def decay_rows(c):
        cf = cum_ref[c, pl.ds(g8, heads), :]
        cb = cum_ref[c, pl.ds(n_fwd + g8, heads), :]
        return cf, cb, dtt_ref[c, pl.ds(g8, heads), :], dtt_ref[c, pl.ds(n_fwd + g8, heads), :]

    def blockdiag(xp):
        zero = jnp.zeros_like(xp)
        return jnp.concatenate([jnp.where(left, xp, zero).astype(BF16), jnp.where(left, zero, xp).astype(BF16)], axis=0)

    def per_head_lanes(col):
        out = jnp.zeros((1, gw), F32)
        for e in range(heads):
            out = jnp.where(lane_head == e, col[e:e + 1, :], out)
        return out

    def init_states(sq, run):
        if has_init:
            for d in range(2):
                for j in range(npairs):
                    pair = jnp.concatenate([s0_ref[sq, d, 2 * j], s0_ref[sq, d, 2 * j + 1]], axis=0)
                    run[d, :, j * pw:(j + 1) * pw] = pair.T
        else:
            run[...] = jnp.zeros(run.shape, F32)

    def local_states(c, run):
        rows = chunk_rows(c)
        bt = bs_ref[rows, :].T
        cf, cb, dtf, dtb = decay_rows(c)
        tot_f = cf[:, SSD_CHUNK - 1:SSD_CHUNK]
        wf = jnp.exp(tot_f - cf) * dtf
        wb = jnp.exp(cb[:, 0:1] - cb) * dtb
        dec_f = per_head_lanes(jnp.exp(tot_f))
        for j in range(npairs):
            cols_j = slice(j * pw, (j + 1) * pw)
            xbd = blockdiag(xs_ref[rows, cols_j])

            def contribution(w):
                lhs = jnp.concatenate([bt * w[2 * j:2 * j + 1, :], bt * w[2 * j + 1:2 * j + 2, :]], axis=1)
                return jnp.dot(lhs.astype(BF16), xbd, preferred_element_type=F32)

            sf = run[0, :, cols_j]
            sprev_ref[c, :, cols_j] = sf.astype(BF16)
            run[0, :, cols_j] = sf * dec_f[:, cols_j] + contribution(wf)
            sloc_ref[c, :, cols_j] = contribution(wb)

    def out_chunk(c, slot, run):
        rows = chunk_rows(c)
        cc = cs_ref[rows, :].astype(BF16)
        bc = bs_ref[rows, :].astype(BF16)
        cbm = lax.dot_general(cc, bc, contract_last, preferred_element_type=F32)
        cf, cb, dtf, dtb = decay_rows(c)
        ex = jnp.exp(jnp.concatenate([cf, cb], axis=0))
        pieces, rem = [], ex
        for _ in range(N_SPLIT):
            p = rem.astype(BF16).astype(F32)
            pieces.append(p)
            rem = rem - p
        filler = jnp.zeros((SSD_CHUNK - 2 * heads * (1 + N_SPLIT), SSD_CHUNK), F32)
        cols = jnp.concatenate([cf, cb] + pieces + [filler], axis=0).T
        ecols = jnp.where(si >= 2 * heads, cols, 0.0).astype(BF16)
        e_all = jnp.dot(ecols, rep_ref[...], preferred_element_type=F32)
        sb = run[1]
        yoff_ref[slot] = (e_all[:, 0:gw] * jnp.dot(cc, sprev_ref[c], preferred_element_type=F32)
                          + e_all[:, gw:2 * gw] * jnp.dot(cc, sb.astype(BF16), preferred_element_type=F32))
        run[1] = sb * per_head_lanes(jnp.exp(cb[:, 0:1])) + sloc_ref[c]
        part = jnp.zeros((SSD_CHUNK, pw), F32)
        for j in range(npairs):
            cols_j = slice(j * pw, (j + 1) * pw)
            xj = xs_ref[rows, cols_j]
            ms = []
            for e in (2 * j, 2 * j + 1):
                arg = jnp.where(lower, cols[:, e:e + 1] - cf[e:e + 1, :],
                                cols[:, heads + e:heads + e + 1] - cb[e:e + 1, :])
                dtf_e, dtb_e = dtf[e:e + 1, :], dtb[e:e + 1, :]
                coef = jnp.where(strict_lower, dtf_e, jnp.where(diag, dtf_e + dtb_e, dtb_e))
                ms.append(cbm * (jnp.exp(arg) * coef))
            lhs = jnp.concatenate(ms, axis=1).astype(BF16)
            yj = (dvec_ref[:, cols_j] * xj + yoff_ref[slot, :, cols_j]
                  + jnp.dot(lhs, blockdiag(xj), preferred_element_type=F32))
            v = yj * _silu(z_ref[rows, cols_j])
            part = part + v * v
            u_ref[rows, cols_j] = (v * nw_ref[:, cols_j]).astype(BF16)

        @pl.when(g == 0)
        def _():
            ss_ref[rows, :] = part

        @pl.when(g != 0)
        def _():
            ss_ref[rows, :] = ss_ref[rows, :] + part

    def emit_states(sq, run):
        for d in range(2):
            for j in range(npairs):
                pair = run[d, :, j * pw:(j + 1) * pw].T
                st_ref[sq, d, 2 * j] = pair[0:hp]
                st_ref[sq, d, 2 * j + 1] = pair[hp:2 * hp]

    for sq in range(seqs_per_step):
        base = sq * nc
        run = run_ref.at[sq % 2]
        init_states(sq, run)

        def ascending(ii, carry, base=base, run=run):
            for k in range(2):
                local_states(base + 2 * ii + k, run)
            return carry

        def descending(ii, carry, base=base, run=run):
            for k in range(2):
                out_chunk(base + nc - 1 - (2 * ii + k), k, run)
            return carry

        lax.fori_loop(0, nc // 2, ascending, 0)
        lax.fori_loop(0, nc // 2, descending, 0)
        if emit_state:
            emit_states(sq, run)


def _ssd_main(z, xbc, cum, dtt, dvec, norm_w, rep, s0, bufs, state_out, *, row_block0, nseq, l, seqs_per_step, layer,
              d_inner):
    t = z.shape[0]
    heads = d_inner // SSD_HEADDIM // SSD_GROUPS
    gw = heads * SSD_HEADDIM
    n = SSD_STATE
    bb = d_inner // n
    cb = bb + SSD_GROUPS
    nc = l // SSD_CHUNK
    sps = seqs_per_step
    rows = sps * l
    has_init = s0 is not None
    emit_state = state_out is not None
    assert nseq % sps == 0 and nc % 2 == 0
    h2 = cum.shape[2]
    cum = cum.reshape(nseq // sps, sps * nc, h2, SSD_CHUNK)
    dtt = dtt.reshape(nseq // sps, sps * nc, h2, SSD_CHUNK)

    def rb(b):
        return row_block0 + b

    def lay(width, off):
        return pl.BlockSpec((None, 1, width), lambda b, g: (layer, 0, off + g))

    state_spec = pl.BlockSpec((sps, None, 2, heads, SSD_HEADDIM, n), lambda b, g: (b, layer, 0, g, 0, 0))
    in_specs = [
        pl.BlockSpec((rows, gw), lambda b, g: (rb(b), g)),
        pl.BlockSpec((rows, gw), lambda b, g: (rb(b), g)),
        pl.BlockSpec((rows, n), lambda b, g: (rb(b), bb + g)),
        pl.BlockSpec((rows, n), lambda b, g: (rb(b), cb + g)),
        pl.BlockSpec((None, sps * nc, h2, SSD_CHUNK), lambda b, g: (b, 0, 0, 0)),
        pl.BlockSpec((None, sps * nc, h2, SSD_CHUNK), lambda b, g: (b, 0, 0, 0)),
        lay(gw, 0),
        lay(gw, 0),
        pl.BlockSpec(rep.shape, lambda b, g: (0, 0)),
    ]
    args = [z, xbc, xbc, xbc, cum, dtt, dvec, norm_w, rep]
    if has_init:
        in_specs.append(state_spec)
        args.append(s0)
    aliases = {}
    out_shape = [jax.ShapeDtypeStruct((t, d_inner), BF16), jax.ShapeDtypeStruct((t, 128), F32)]
    out_specs = [pl.BlockSpec((rows, gw), lambda b, g: (rb(b), g)), pl.BlockSpec((rows, 128), lambda b, g: (rb(b), 0))]
    any_spec = pl.BlockSpec(memory_space=pl.ANY)

    def alias(buf, out_idx):
        aliases[len(args)] = out_idx
        in_specs.append(any_spec)
        args.append(buf)

    if bufs is not None:
        alias(bufs[0], 0)
        alias(bufs[1], 1)
    if emit_state:
        if not isinstance(state_out, jax.ShapeDtypeStruct):
            alias(state_out, 2)
        out_shape.append(jax.ShapeDtypeStruct(state_out.shape, F32))
        out_specs.append(state_spec)
    return pl.pallas_call(
        functools.partial(_ssd_main_body, l=l, seqs_per_step=sps, has_init=has_init, emit_state=emit_state,
                          n_aliased=len(aliases)),
        grid=(nseq // sps, SSD_GROUPS),
        in_specs=in_specs,
        out_specs=out_specs,
        out_shape=out_shape,
        input_output_aliases=aliases,
        scratch_shapes=[
            pltpu.VMEM((sps * nc, n, gw), F32),
            pltpu.VMEM((sps * nc, n, gw), BF16),
            pltpu.VMEM((2, 2, n, gw), F32),
            pltpu.VMEM((2, SSD_CHUNK, gw), F32),
        ],
        compiler_params=_params(("arbitrary", "arbitrary")),
        name="ssd_scan_gate",
    )(*args)


def _head_lane_spreader(heads, gw):
    hd2 = 2 * heads
    rows = jnp.arange(SSD_CHUNK)[:, None]
    lanes = jnp.arange(2 * gw)[None, :]
    hit = (rows >= hd2) & (rows < hd2 * (1 + N_SPLIT)) & ((rows % hd2) == lanes // SSD_HEADDIM)
    return hit.astype(BF16)


def _attn_body(*refs, l, tq, has_cache, rope, emit_kv, scale, n_aliased):
    it = iter(refs)
    q_ref, k_ref, v_ref, qn_ref, kn_ref = [next(it) for _ in range(5)]
    cos_ref, sin_ref = (next(it), next(it)) if rope else (None, None)
    ck_ref, cv_ref = (next(it), next(it)) if has_cache else (None, None)
    for _ in range(n_aliased):
        next(it)
    o_ref = next(it)
    ko_ref, vo_ref = (next(it), next(it)) if emit_kv else (None, None)
    kb_ref = next(it)

    lane = lax.broadcasted_iota(jnp.int32, (1, HEAD_DIM), 1)
    even = (lane % 2) == 0
    contract_last = (((1,), (1,)), ((), ()))

    def norm_rope(x, w, rows):
        r = lax.rsqrt(jnp.mean(x * x, axis=-1, keepdims=True) + EPS)
        xn = (x * r) * w
        if rope:
            partner = jnp.where(even, pltpu.roll(xn, HEAD_DIM - 1, 1), pltpu.roll(xn, 1, 1))
            xn = xn * cos_ref[rows, :] + partner * sin_ref[rows, :]
        return xn

    def kv_head(kv):
        kcols = slice(kv * HEAD_DIM, (kv + 1) * HEAD_DIM)
        for c in range(l // tq):
            rows = slice(c * tq, (c + 1) * tq)
            kn = norm_rope(k_ref[rows, kcols], kn_ref[...], rows)
            kb_ref[rows, kcols] = kn.astype(BF16)
            if emit_kv:
                ko_ref[rows, kcols] = kn
        v = v_ref[:, kcols]
        if emit_kv:
            vo_ref[:, kcols] = v
        vb = v.astype(BF16)
        kb = kb_ref[:, kcols]
        if has_cache:
            ckb = ck_ref[:, kcols].astype(BF16)
            cvb = cv_ref[:, kcols].astype(BF16)

        for c in range(l // tq):
            rows = slice(c * tq, (c + 1) * tq)
            for hh in range(GQA_GROUP):
                head = kv * GQA_GROUP + hh
                cols = slice(head * HEAD_DIM, (head + 1) * HEAD_DIM)
                qn = norm_rope(q_ref[rows, cols], qn_ref[...], rows).astype(BF16)
                s = lax.dot_general(qn, kb, contract_last, preferred_element_type=F32) * scale
                mx = jnp.max(s, axis=-1, keepdims=True)
                if has_cache:
                    sc = lax.dot_general(qn, ckb, contract_last, preferred_element_type=F32) * scale
                    mx = jnp.maximum(mx, jnp.max(sc, axis=-1, keepdims=True))
                p = jnp.exp(s - mx)
                den = jnp.sum(p, axis=-1, keepdims=True)
                o = jnp.dot(p.astype(BF16), vb, preferred_element_type=F32)
                if has_cache:
                    pc = jnp.exp(sc - mx)
                    den = den + jnp.sum(pc, axis=-1, keepdims=True)
                    o = o + jnp.dot(pc.astype(BF16), cvb, preferred_element_type=F32)
                o_ref[rows, cols] = (o / den).astype(o_ref.dtype)

    for kv in range(k_ref.shape[1] // HEAD_DIM):
        kv_head(kv)


def _attention(qkv, q_norm, k_norm, tables, caches, o_buf, kv_out, *, row_block0, nseq, l, layer, n_heads, n_kv,
               kv_per_step):
    t = qkv.shape[0]
    kw = kv_per_step * HEAD_DIM
    gq = GQA_GROUP * kw
    assert n_kv % kv_per_step == 0 and n_heads % kv_per_step == 0
    kcol = n_heads // kv_per_step
    vcol = (n_heads + n_kv) // kv_per_step
    tq = min(l, 256)

    def rb(b):
        return row_block0 + b

    norm_spec = pl.BlockSpec((None, 1, HEAD_DIM), lambda b, j: (layer, 0, 0))
    in_specs = [
        pl.BlockSpec((l, gq), lambda b, j: (rb(b), j)),
        pl.BlockSpec((l, kw), lambda b, j: (rb(b), kcol + j)),
        pl.BlockSpec((l, kw), lambda b, j: (rb(b), vcol + j)),
        norm_spec, norm_spec,
    ]
    args = [qkv, qkv, qkv, q_norm, k_norm]
    if tables is not None:
        in_specs += [pl.BlockSpec((l, HEAD_DIM), lambda b, j: (0, 0))] * 2
        args += list(tables)
    if caches is not None:
        past = caches[0].shape[2]
        in_specs += [pl.BlockSpec((None, None, past, kw), lambda b, j: (b, layer, 0, j))] * 2
        args += list(caches)
    any_spec = pl.BlockSpec(memory_space=pl.ANY)
    aliases = {}

    def alias(buf, out_idx):
        aliases[len(args)] = out_idx
        in_specs.append(any_spec)
        args.append(buf)

    if o_buf is not None:
        alias(o_buf, 0)
    out_shape = [jax.ShapeDtypeStruct((t, n_heads * HEAD_DIM), BF16)]
    out_specs = [pl.BlockSpec((l, gq), lambda b, j: (rb(b), j))]
    if kv_out is not None:
        fresh = isinstance(kv_out, jax.ShapeDtypeStruct)
        for k in range(2):
            if not fresh:
                alias(kv_out[k], 1 + k)
            out_shape.append(jax.ShapeDtypeStruct(kv_out.shape if fresh else kv_out[k].shape, F32))
            out_specs.append(pl.BlockSpec((None, None, l, kw), lambda b, j: (b, layer, 0, j)))
    return pl.pallas_call(
        functools.partial(_attn_body, l=l, tq=tq, has_cache=caches is not None, rope=tables is not None,
                          emit_kv=kv_out is not None, scale=HEAD_DIM ** -0.5, n_aliased=len(aliases)),
        grid=(nseq, n_kv // kv_per_step),
        in_specs=in_specs,
        out_specs=out_specs,
        out_shape=out_shape,
        input_output_aliases=aliases,
        scratch_shapes=[pltpu.VMEM((l, kw), BF16)],
        compiler_params=_params(("arbitrary", "arbitrary")),
        name="gqa_attention",
    )(*args)


def _rope_tables(n_tok):
    pairs = HEAD_DIM // 4
    rows = n_tok // GRID_W
    row_id = jnp.broadcast_to(jnp.arange(rows, dtype=F32)[:, None], (rows, GRID_W)).reshape(-1)
    col_id = jnp.broadcast_to(jnp.arange(GRID_W, dtype=F32)[None, :], (rows, GRID_W)).reshape(-1)
    inv = ROPE_THETA ** (-jnp.arange(pairs, dtype=F32) / pairs)
    ang = jnp.concatenate([row_id[:, None] * inv, col_id[:, None] * inv], axis=-1)
    cos, sin = jnp.cos(ang), jnp.sin(ang)
    cos_full = jnp.repeat(cos, 2, axis=-1)
    sin_signed = jnp.stack([-sin, sin], axis=-1).reshape(n_tok, HEAD_DIM)
    return cos_full, sin_signed


def kernel(x_prompt, x_sample, c, state_ssd, cache_k, cache_v, c_ctx, ada_w, ada_b, norm_pre, norm_post, ffn_w1, ffn_w3, ffn_w2, ssd_in_w, ssd_conv_w, ssd_conv_b, ssd_dt_bias, ssd_a_log, ssd_d, ssd_norm_w, ssd_out_w, attn_qkv_w, attn_q_norm, attn_k_norm, attn_o_w):
    batch, seq, d = x_prompt.shape
    dec_batch, dec_seq, _ = x_sample.shape
    depth = ada_w.shape[0]
    n_ctx = batch * seq
    n_lat = dec_batch * dec_seq
    t = n_ctx + n_lat
    d_inner = ssd_out_w.shape[1]
    n_ssd_heads = ssd_d.shape[1]
    n_kv = cache_k.shape[3]
    n_heads = attn_o_w.shape[1] // HEAD_DIM
    n_ssd_layers = ssd_in_w.shape[0]
    n_attn_layers = attn_qkv_w.shape[0]
    past = cache_k.shape[2]
    assert dec_seq % ROW_TILE == 0 and n_ctx % ROW_TILE == 0 and dec_batch + 1 <= N_COND_ROWS
    ctx_rb0, lat_rb0 = 0, n_ctx // dec_seq

    cond = jnp.concatenate([c_ctx[None, :], c, jnp.zeros((N_COND_ROWS - 1 - dec_batch, d), F32)], axis=0)
    modr = _modulation(cond, ada_w, ada_b).reshape(depth * N_COND_ROWS * N_MOD, 1, d)
    norm_pre_r = norm_pre.reshape(depth, 3, 1, d)
    norm_post_r = norm_post.reshape(depth, 3, 1, d)

    def postpre(xs, y, post, pre):
        post_a = pre_a = None
        if post is not None:
            l, j, res_w = post
            post_a = (norm_post_r[l, j], (l * N_COND_ROWS) * N_MOD + 3 * j, res_w)
        if pre is not None:
            l, j = pre
            pre_a = (norm_pre_r[l, j], (l * N_COND_ROWS) * N_MOD + 3 * j)
        outs = _postpre(xs[0], xs[1], y, modr, rows_per_cond_b=dec_seq, post=post_a, pre=pre_a)
        new_xs = (outs[0], outs[1]) if post is not None else xs
        return new_xs, (outs[-1] if pre is not None else None)

    def ffn(h, l, k):
        a = _gateup(h, ffn_w1, ffn_w3, (l, k), tm=ROW_TILE, tn=512)
        return _mm(a, ffn_w2, (l, k), tm=512, tn=512, out_dtype=F32)

    cos_sin = _rope_tables(dec_seq)
    cache_k2 = cache_k.reshape(dec_batch, n_attn_layers, past, n_kv * HEAD_DIM)
    cache_v2 = cache_v.reshape(dec_batch, n_attn_layers, past, n_kv * HEAD_DIM)
    q_norm_r = attn_q_norm.reshape(n_attn_layers, 1, HEAD_DIM)
    k_norm_r = attn_k_norm.reshape(n_attn_layers, 1, HEAD_DIM)
    new_kv = jax.ShapeDtypeStruct((batch, n_attn_layers, seq, n_kv * HEAD_DIM), F32)
    new_state = jax.ShapeDtypeStruct((batch, n_ssd_layers, 2, n_ssd_heads, SSD_HEADDIM, SSD_STATE), F32)
    conv_dim = ssd_conv_w.shape[-1]
    dt_width = 2 * n_ssd_heads
    assert seq % SSD_CHUNK == 0 and ROW_TILE % seq == 0 and dec_seq == ROW_TILE
    assert d_inner % 1024 == 0 and conv_dim % 1024 == 0 and (d_inner + conv_dim) % dt_width == 0
    conv_b_r = ssd_conv_b.reshape(n_ssd_layers, 1, conv_dim)
    dvec_r = jnp.repeat(ssd_d, SSD_HEADDIM, axis=-1).reshape(n_ssd_layers, 1, d_inner)
    ssd_nw_r = ssd_norm_w.reshape(n_ssd_layers, 1, d_inner)
    group_heads = n_ssd_heads // SSD_GROUPS
    rep = _head_lane_spreader(group_heads, group_heads * SSD_HEADDIM)
    n_mix = depth // n_attn_layers

    xs = (x_prompt.reshape(n_ctx, d), x_sample.reshape(n_lat, d))
    _, h = postpre(xs, None, None, (0, 0))
    for l in range(depth):
        y = ffn(h, l, 0)
        xs, h = postpre(xs, y, (l, 0, 0.5), (l, 1))
        i = l // n_mix
        if l % n_mix == 0:
            z = _mm(h, ssd_in_w, (i,), tm=ROW_TILE, tn=1024, out_dtype=F32, n_out=d_inner)
            xbc = _mm_conv(h, ssd_in_w, (i,), ssd_conv_w, conv_b_r, i, tm=ROW_TILE, tn=1024, col0=d_inner // 1024,
                           n_out=conv_dim, seq=seq, n_short_tiles=n_ctx // ROW_TILE)
            dt_raw = _mm(h, ssd_in_w, (i,), tm=ROW_TILE, tn=dt_width, out_dtype=F32,
                         col0=(d_inner + conv_dim) // dt_width, n_out=dt_width)
            cum, dtt = _ssd_prep(dt_raw, ssd_dt_bias[i], ssd_a_log[i], row_block0=ctx_rb0, nseq=batch, l=seq)
            u, ss, new_state = _ssd_main(z, xbc, cum, dtt, dvec_r, ssd_nw_r, rep, None, None, new_state,
                                         row_block0=ctx_rb0, nseq=batch, l=seq, seqs_per_step=ROW_TILE // seq,
                                         layer=i, d_inner=d_inner)
            cum, dtt = _ssd_prep(dt_raw, ssd_dt_bias[i], ssd_a_log[i], row_block0=lat_rb0, nseq=dec_batch, l=dec_seq)
            u, ss = _ssd_main(z, xbc, cum, dtt, dvec_r, ssd_nw_r, rep, state_ssd, (u, ss), None,
                              row_block0=lat_rb0, nseq=dec_batch, l=dec_seq, seqs_per_step=1, layer=i,
                              d_inner=d_inner)
            y = _mm(u, ssd_out_w, (i,), tm=ROW_TILE, tn=512, out_dtype=F32, ss=ss, inv_width=1.0 / d_inner)
        else:
            qkv = _mm(h, attn_qkv_w, (i,), tm=ROW_TILE, tn=1024, out_dtype=F32)
            o, *new_kv = _attention(qkv, q_norm_r, k_norm_r, None, None, None, new_kv,
                                    row_block0=ctx_rb0, nseq=batch, l=seq, layer=i, n_heads=n_heads, n_kv=n_kv,
                                    kv_per_step=n_kv)
            (o,) = _attention(qkv, q_norm_r, k_norm_r, cos_sin, (cache_k2, cache_v2), o, None,
                              row_block0=lat_rb0, nseq=dec_batch, l=dec_seq, layer=i, n_heads=n_heads, n_kv=n_kv,
                              kv_per_step=1)
            y = _mm(o, attn_o_w, (i,), tm=ROW_TILE, tn=1024, out_dtype=F32)
        xs, h = postpre(xs, y, (l, 1, 1.0), (l, 2))
        y = ffn(h, l, 1)
        xs, h = postpre(xs, y, (l, 2, 0.5), (l + 1, 0) if l + 1 < depth else None)

    y_prompt = xs[0].reshape(batch, seq, d)
    y_sample = xs[1].reshape(dec_batch, dec_seq, d)
    new_cache_k = new_kv[0].reshape(batch, n_attn_layers, seq, n_kv, HEAD_DIM)
    new_cache_v = new_kv[1].reshape(batch, n_attn_layers, seq, n_kv, HEAD_DIM)
    return (y_prompt, y_sample, new_state.astype(x_prompt.dtype), new_cache_k, new_cache_v)
```

```python
import functools

import jax
import jax.numpy as jnp
from jax import lax
from jax.experimental import pallas as pl
from jax.experimental.pallas import tpu as pltpu

F32 = jnp.float32
BF16 = jnp.bfloat16

N_MOD = 9
N_COND_ROWS = 8
SSD_HEADDIM = 64
SSD_GROUPS = 8
SSD_STATE = 128
SSD_CHUNK = 128
D_CONV = 5
HEAD_DIM = 128
GQA_GROUP = 4
GRID_W = 64
ROPE_THETA = 10000.0
EPS = 1e-6

V7X_VMEM_LIMIT_BYTES = 56 * 1024 * 1024
ROW_TILE = 1024
NORM_ROW_TILE = 512
W_CAST_ROWS = 256
CONV_COL_CHUNK = 256
N_SPLIT = 3


def _params(sem):
    return pltpu.CompilerParams(dimension_semantics=sem, vmem_limit_bytes=V7X_VMEM_LIMIT_BYTES)


def _silu(x):
    h = 0.5 * x
    return h + h * jnp.tanh(h)


def _cast_weight(w_ref, wb_ref):
    k = w_ref.shape[0]
    rows = W_CAST_ROWS if k % W_CAST_ROWS == 0 else k

    def body(r, carry):
        sl = pl.ds(pl.multiple_of(r * rows, rows), rows)
        wb_ref[sl, :] = w_ref[sl, :].astype(BF16)
        return carry

    lax.fori_loop(0, k // rows, body, 0)


def _weight_spec(w, widx, k, tn, col0=0):
    lead = tuple(widx)
    return pl.BlockSpec((None,) * len(lead) + (k, tn), lambda j, i: lead + (0, col0 + j))


def _mod_body(c_ref, w_ref, b_ref, o_ref):
    s = _silu(c_ref[...]).astype(BF16)
    o_ref[...] = jnp.dot(s, w_ref[...].astype(BF16), preferred_element_type=F32) + b_ref[...]


def _modulation(cond, ada_w, ada_b):
    depth, d, n = ada_w.shape
    tn = 1024
    return pl.pallas_call(
        _mod_body,
        grid=(depth, n // tn),
        in_specs=[
            pl.BlockSpec((N_COND_ROWS, d), lambda l, j: (0, 0)),
            pl.BlockSpec((None, d, tn), lambda l, j: (l, 0, j)),
            pl.BlockSpec((None, 1, tn), lambda l, j: (l, 0, j)),
        ],
        out_specs=pl.BlockSpec((None, N_COND_ROWS, tn), lambda l, j: (l, 0, j)),
        out_shape=jax.ShapeDtypeStruct((depth, N_COND_ROWS, n), F32),
        compiler_params=_params(("arbitrary", "arbitrary")),
        name="adaln_modulation",
    )(cond, ada_w, ada_b.reshape(depth, 1, n))


def _postpre_body(*refs, has_post, has_pre, res_w, n_first):
    it = iter(refs)
    xa_ref, xb_ref = next(it), next(it)
    if has_post:
        y_ref, gpost_ref, gate_ref = next(it), next(it), next(it)
    if has_pre:
        gpre_ref, scale_ref, shift_ref = next(it), next(it), next(it)
    if has_post:
        xao_ref, xbo_ref = next(it), next(it)
    if has_pre:
        h_ref = next(it)

    def step(x_ref, xo_ref):
        x = x_ref[...]
        if has_post:
            y = y_ref[...]
            r = lax.rsqrt(jnp.mean(y * y, axis=-1, keepdims=True) + EPS)
            x = x + (res_w * gate_ref[...]) * ((y * r) * gpost_ref[...])
            xo_ref[...] = x
        if has_pre:
            r = lax.rsqrt(jnp.mean(x * x, axis=-1, keepdims=True) + EPS)
            h = ((x * r) * gpre_ref[...]) * (1.0 + scale_ref[...]) + shift_ref[...]
            h_ref[...] = h.astype(BF16)

    first = pl.program_id(0) < n_first

    @pl.when(first)
    def _():
        step(xa_ref, xao_ref if has_post else None)

    @pl.when(jnp.logical_not(first))
    def _():
        step(xb_ref, xbo_ref if has_post else None)


def _postpre(xa, xb, y, modr, *, rows_per_cond_b, post, pre):
    (ra, d), rb = xa.shape, xb.shape[0]
    tm = NORM_ROW_TILE
    na, nb = ra // tm, rb // tm

    def cond_row(i):
        return jnp.where(i < na, 0, 1 + (jnp.maximum(i - na, 0) * tm) // rows_per_cond_b)

    def mod_spec(base, k):
        return pl.BlockSpec((None, 1, d), lambda i: (base + cond_row(i) * N_MOD + k, 0, 0))

    a_spec = pl.BlockSpec((tm, d), lambda i: (jnp.minimum(i, na - 1), 0))
    b_spec = pl.BlockSpec((tm, d), lambda i: (jnp.maximum(i - na, 0), 0))
    tok_spec = pl.BlockSpec((tm, d), lambda i: (i, 0))
    vec_spec = pl.BlockSpec((1, d), lambda i: (0, 0))
    args, in_specs, out_shape, out_specs = [xa, xb], [a_spec, b_spec], [], []
    res_w = 0.0
    if post is not None:
        g_post, base, res_w = post
        args += [y, g_post, modr]
        in_specs += [tok_spec, vec_spec, mod_spec(base, 2)]
        out_shape += [jax.ShapeDtypeStruct((ra, d), F32), jax.ShapeDtypeStruct((rb, d), F32)]
        out_specs += [a_spec, b_spec]
    if pre is not None:
        g_pre, base = pre
        args += [g_pre, modr, modr]
        in_specs += [vec_spec, mod_spec(base, 1), mod_spec(base, 0)]
        out_shape.append(jax.ShapeDtypeStruct((ra + rb, d), BF16))
        out_specs.append(tok_spec)
    return pl.pallas_call(
        functools.partial(_postpre_body, has_post=post is not None, has_pre=pre is not None, res_w=res_w, n_first=na),
        grid=(na + nb,),
        in_specs=in_specs,
        out_specs=out_specs,
        out_shape=out_shape,
        compiler_params=_params(("arbitrary",)),
        name="norm_modulate_residual",
    )(*args)


def _mm_body(*refs, has_scale, inv_width):
    if has_scale:
        x_ref, w_ref, ss_ref, o_ref, wb_ref = refs
    else:
        x_ref, w_ref, o_ref, wb_ref = refs

    @pl.when(pl.program_id(1) == 0)
    def _():
        _cast_weight(w_ref, wb_ref)

    acc = jnp.dot(x_ref[...], wb_ref[...], preferred_element_type=F32)
    if has_scale:
        ms = jnp.sum(ss_ref[...], axis=-1, keepdims=True) * inv_width
        acc = acc * lax.rsqrt(ms + EPS)
    o_ref[...] = acc.astype(o_ref.dtype)


def _mm(x, w, widx, *, tm, tn, out_dtype, ss=None, inv_width=1.0, col0=0, n_out=None):
    t, k = x.shape
    n = w.shape[-1] if n_out is None else n_out
    args = [x, w]
    in_specs = [pl.BlockSpec((tm, k), lambda j, i: (i, 0)), _weight_spec(w, widx, k, tn, col0)]
    if ss is not None:
        args.append(ss)
        in_specs.append(pl.BlockSpec((tm, ss.shape[1]), lambda j, i: (i, 0)))
    return pl.pallas_call(
        functools.partial(_mm_body, has_scale=ss is not None, inv_width=inv_width),
        grid=(n // tn, t // tm),
        in_specs=in_specs,
        out_specs=pl.BlockSpec((tm, tn), lambda j, i: (i, j)),
        out_shape=jax.ShapeDtypeStruct((t, n), out_dtype),
        scratch_shapes=[pltpu.VMEM((k, tn), BF16)],
        compiler_params=_params(("arbitrary", "arbitrary")),
        name="matmul_ws",
    )(*args)


def _gateup_body(x_ref, w1_ref, w3_ref, o_ref, w1b_ref, w3b_ref):
    @pl.when(pl.program_id(1) == 0)
    def _():
        _cast_weight(w1_ref, w1b_ref)
        _cast_weight(w3_ref, w3b_ref)

    x = x_ref[...]
    a = jnp.dot(x, w1b_ref[...], preferred_element_type=F32)
    b = jnp.dot(x, w3b_ref[...], preferred_element_type=F32)
    o_ref[...] = (_silu(a) * b).astype(o_ref.dtype)


def _gateup(x, w1, w3, widx, *, tm, tn):
    t, k = x.shape
    n = w1.shape[-1]
    return pl.pallas_call(
        _gateup_body,
        grid=(n // tn, t // tm),
        in_specs=[pl.BlockSpec((tm, k), lambda j, i: (i, 0)), _weight_spec(w1, widx, k, tn),
                  _weight_spec(w3, widx, k, tn)],
        out_specs=pl.BlockSpec((tm, tn), lambda j, i: (i, j)),
        out_shape=jax.ShapeDtypeStruct((t, n), BF16),
        scratch_shapes=[pltpu.VMEM((k, tn), BF16), pltpu.VMEM((k, tn), BF16)],
        compiler_params=_params(("arbitrary", "arbitrary")),
        name="ffn_gate_up",
    )(x, w1, w3)


def _mm_conv_body(x_ref, w_ref, cw_ref, cb_ref, o_ref, wb_ref, *, seq, n_short_tiles):
    i = pl.program_id(1)

    @pl.when(i == 0)
    def _():
        _cast_weight(w_ref, wb_ref)

    tm, tn = o_ref.shape
    x = x_ref[...]
    keep_halo = (i >= n_short_tiles).astype(F32)
    nseg = tm // seq
    win_rows = seq + 16
    zero8 = jnp.zeros((8, CONV_COL_CHUNK), F32)
    for q in range(tn // CONV_COL_CHUNK):
        cols = slice(q * CONV_COL_CHUNK, (q + 1) * CONV_COL_CHUNK)
        raw = jnp.dot(x, wb_ref[:, cols], preferred_element_type=F32)
        wq = cw_ref[:, cols]
        bq = cb_ref[:, cols]
        for s in range(nseg):
            seg = raw[s * seq:(s + 1) * seq]
            top = raw[s * seq - 8:s * seq] * keep_halo if s > 0 else zero8
            bot = raw[(s + 1) * seq:(s + 1) * seq + 8] * keep_halo if s < nseg - 1 else zero8
            win = jnp.concatenate([top, seg, bot], axis=0)
            acc = bq + wq[D_CONV // 2:D_CONV // 2 + 1, :] * seg
            for k in range(D_CONV):
                shift = k - D_CONV // 2
                if shift != 0:
                    moved = pltpu.roll(win, (-shift) % win_rows, 0)
                    acc = acc + wq[k:k + 1, :] * moved[8:8 + seq]
            o_ref[s * seq:(s + 1) * seq, cols] = _silu(acc)


def _mm_conv(x, w, widx, conv_w, conv_b, layer, *, tm, tn, col0, n_out, seq, n_short_tiles):
    t, k = x.shape
    return pl.pallas_call(
        functools.partial(_mm_conv_body, seq=seq, n_short_tiles=n_short_tiles),
        grid=(n_out // tn, t // tm),
        in_specs=[
            pl.BlockSpec((tm, k), lambda j, i: (i, 0)),
            _weight_spec(w, widx, k, tn, col0),
            pl.BlockSpec((None, D_CONV, tn), lambda j, i: (layer, 0, j)),
            pl.BlockSpec((None, 1, tn), lambda j, i: (layer, 0, j)),
        ],
        out_specs=pl.BlockSpec((tm, tn), lambda j, i: (i, j)),
        out_shape=jax.ShapeDtypeStruct((t, n_out), F32),
        scratch_shapes=[pltpu.VMEM((k, tn), BF16)],
        compiler_params=_params(("arbitrary", "arbitrary")),
        name="matmul_conv_silu",
    )(x, w, conv_w, conv_b)


def _ssd_prep_body(dt_ref, dtb_ref, alog_ref, cum_ref, dtt_ref, *, nc):
    a_coef = -jnp.exp(alog_ref[...])
    q = lax.broadcasted_iota(jnp.int32, (SSD_CHUNK, SSD_CHUNK), 0)
    s = lax.broadcasted_iota(jnp.int32, (SSD_CHUNK, SSD_CHUNK), 1)
    lower = (q >= s).astype(F32)
    upper = (q <= s).astype(F32)
    n_fwd = dt_ref.shape[1] // 2
    for c in range(nc):
        x = dt_ref[c * SSD_CHUNK:(c + 1) * SSD_CHUNK, :] + dtb_ref[...]
        dt = jnp.maximum(x, 0.0) + jnp.log1p(jnp.exp(-jnp.abs(x)))
        a = dt * a_coef
        prefix = jnp.dot(lower, a, preferred_element_type=F32, precision=lax.Precision.HIGHEST)
        suffix = jnp.dot(upper, a, preferred_element_type=F32, precision=lax.Precision.HIGHEST)
        cum = jnp.where(s < n_fwd, prefix, suffix)
        cum_ref[c] = cum.T
        dtt_ref[c] = dt.T


def _ssd_prep(dt_raw, dt_bias, a_log, *, row_block0, nseq, l):
    nc = l // SSD_CHUNK
    h2 = dt_bias.size
    out = jax.ShapeDtypeStruct((nseq, nc, h2, SSD_CHUNK), F32)
    out_spec = pl.BlockSpec((None, nc, h2, SSD_CHUNK), lambda b: (b, 0, 0, 0))
    return pl.pallas_call(
        functools.partial(_ssd_prep_body, nc=nc),
        grid=(nseq,),
        in_specs=[
            pl.BlockSpec((l, h2), lambda b: (row_block0 + b, 0)),
            pl.BlockSpec((1, h2), lambda b: (0, 0)),
            pl.BlockSpec((1, h2), lambda b: (0, 0)),
        ],
        out_specs=[out_spec, out_spec],
        out_shape=[out, out],
        compiler_params=_params(("arbitrary",)),
        name="ssd_decay_prep",
    )(dt_raw, dt_bias.reshape(1, h2), a_log.reshape(1, h2))


def _ssd_main_body(*refs, l, seqs_per_step, has_init, emit_state, n_aliased):
    it = iter(refs)
    z_ref, xs_ref, bs_ref, cs_ref, cum_ref, dtt_ref = [next(it) for _ in range(6)]
    dvec_ref, nw_ref, rep_ref = next(it), next(it), next(it)
    s0_ref = next(it) if has_init else None
    for _ in range(n_aliased):
        next(it)
    u_ref, ss_ref = next(it), next(it)
    st_ref = next(it) if emit_state else None
    sloc_ref, sprev_ref, run_ref, yoff_ref = [next(it) for _ in range(4)]

    nc = l // SSD_CHUNK
    g = pl.program_id(1)
    hp = SSD_HEADDIM
    gw = xs_ref.shape[1]
    heads = gw // hp
    pw = 2 * hp
    npairs = heads // 2
    n_fwd = cum_ref.shape[1] // 2
    g8 = pl.multiple_of(g * heads, heads)
    contract_last = (((1,), (1,)), ((), ()))

    qi = lax.broadcasted_iota(jnp.int32, (SSD_CHUNK, SSD_CHUNK), 0)
    si = lax.broadcasted_iota(jnp.int32, (SSD_CHUNK, SSD_CHUNK), 1)
    lower, strict_lower, diag = qi >= si, qi > si, qi == si
    left = si < hp
    lane_head = lax.broadcasted_iota(jnp.int32, (1, gw), 1) // hp

    def chunk_rows(c):
        return pl.ds(pl.multiple_of(c * SSD_CHUNK, SSD_CHUNK), SSD_CHUNK)

    def decay_rows(c):
        cf = cum_ref[c, pl.ds(g8, heads), :]
        cb = cum_ref[c, pl.ds(n_fwd + g8, heads), :]
        return cf, cb, dtt_ref[c, pl.ds(g8, heads), :], dtt_ref[c, pl.ds(n_fwd + g8, heads), :]

    def blockdiag(xp):
        zero = jnp.zeros_like(xp)
        return jnp.concatenate([jnp.where(left, xp, zero).astype(BF16), jnp.where(left, zero, xp).astype(BF16)], axis=0)

    def per_head_lanes(col):
        out = jnp.zeros((1, gw), F32)
        for e in range(heads):
            out = jnp.where(lane_head == e, col[e:e + 1, :], out)
        return out

    def init_states(sq, run):
        if has_init:
            for d in range(2):
                for j in range(npairs):
                    pair = jnp.concatenate([s0_ref[sq, d, 2 * j], s0_ref[sq, d, 2 * j + 1]], axis=0)
                    run[d, :, j * pw:(j + 1) * pw] = pair.T
        else:
            run[...] = jnp.zeros(run.shape, F32)

    def local_states(c, run):
        rows = chunk_rows(c)
        bt = bs_ref[rows, :].T
        cf, cb, dtf, dtb = decay_rows(c)
        tot_f = cf[:, SSD_CHUNK - 1:SSD_CHUNK]
        wf = jnp.exp(tot_f - cf) * dtf
        wb = jnp.exp(cb[:, 0:1] - cb) * dtb
        dec_f = per_head_lanes(jnp.exp(tot_f))
        for j in range(npairs):
            cols_j = slice(j * pw, (j + 1) * pw)
            xbd = blockdiag(xs_ref[rows, cols_j])

            def contribution(w):
                lhs = jnp.concatenate([bt * w[2 * j:2 * j + 1, :], bt * w[2 * j + 1:2 * j + 2, :]], axis=1)
                return jnp.dot(lhs.astype(BF16), xbd, preferred_element_type=F32)

            sf = run[0, :, cols_j]
            sprev_ref[c, :, cols_j] = sf.astype(BF16)
            run[0, :, cols_j] = sf * dec_f[:, cols_j] + contribution(wf)
            sloc_ref[c, :, cols_j] = contribution(wb)

    def out_chunk(c, slot, run):
        rows = chunk_rows(c)
        cc = cs_ref[rows, :].astype(BF16)
        bc = bs_ref[rows, :].astype(BF16)
        cbm = lax.dot_general(cc, bc, contract_last, preferred_element_type=F32)
        cf, cb, dtf, dtb = decay_rows(c)
        ex = jnp.exp(jnp.concatenate([cf, cb], axis=0))
        pieces, rem = [], ex
        for _ in range(N_SPLIT):
            p = rem.astype(BF16).astype(F32)
            pieces.append(p)
            rem = rem - p
        filler = jnp.zeros((SSD_CHUNK - 2 * heads * (1 + N_SPLIT), SSD_CHUNK), F32)
        cols = jnp.concatenate([cf, cb] + pieces + [filler], axis=0).T
        ecols = jnp.where(si >= 2 * heads, cols, 0.0).astype(BF16)
        e_all = jnp.dot(ecols, rep_ref[...], preferred_element_type=F32)
        sb = run[1]
        yoff_ref[slot] = (e_all[:, 0:gw] * jnp.dot(cc, sprev_ref[c], preferred_element_type=F32)
                          + e_all[:, gw:2 * gw] * jnp.dot(cc, sb.astype(BF16), preferred_element_type=F32))
        run[1] = sb * per_head_lanes(jnp.exp(cb[:, 0:1])) + sloc_ref[c]
        part = jnp.zeros((SSD_CHUNK, pw), F32)
        for j in range(npairs):
            cols_j = slice(j * pw, (j + 1) * pw)
            xj = xs_ref[rows, cols_j]
            ms = []
            for e in (2 * j, 2 * j + 1):
                arg = jnp.where(lower, cols[:, e:e + 1] - cf[e:e + 1, :],
                                cols[:, heads + e:heads + e + 1] - cb[e:e + 1, :])
                dtf_e, dtb_e = dtf[e:e + 1, :], dtb[e:e + 1, :]
                coef = jnp.where(strict_lower, dtf_e, jnp.where(diag, dtf_e + dtb_e, dtb_e))
                ms.append(cbm * (jnp.exp(arg) * coef))
            lhs = jnp.concatenate(ms, axis=1).astype(BF16)
            yj = (dvec_ref[:, cols_j] * xj + yoff_ref[slot, :, cols_j]
                  + jnp.dot(lhs, blockdiag(xj), preferred_element_type=F32))
            v = yj * _silu(z_ref[rows, cols_j])
            part = part + v * v
            u_ref[rows, cols_j] = (v * nw_ref[:, cols_j]).astype(BF16)

        @pl.when(g == 0)
        def _():
            ss_ref[rows, :] = part

        @pl.when(g != 0)
        def _():
            ss_ref[rows, :] = ss_ref[rows, :] + part

    def emit_states(sq, run):
        for d in range(2):
            for j in range(npairs):
                pair = run[d, :, j * pw:(j + 1) * pw].T
                st_ref[sq, d, 2 * j] = pair[0:hp]
                st_ref[sq, d, 2 * j + 1] = pair[hp:2 * hp]

    for sq in range(seqs_per_step):
        base = sq * nc
        run = run_ref.at[sq % 2]
        init_states(sq, run)

        def ascending(ii, carry, base=base, run=run):
            for k in range(2):
                local_states(base + 2 * ii + k, run)
            return carry

        def descending(ii, carry, base=base, run=run):
            for k in range(2):
                out_chunk(base + nc - 1 - (2 * ii + k), k, run)
            return carry

        lax.fori_loop(0, nc // 2, ascending, 0)
        lax.fori_loop(0, nc // 2, descending, 0)
        if emit_state:
            emit_states(sq, run)


def _ssd_main(z, xbc, cum, dtt, dvec, norm_w, rep, s0, bufs, state_out, *, row_block0, nseq, l, seqs_per_step, layer,
              d_inner):
    t = z.shape[0]
    heads = d_inner // SSD_HEADDIM // SSD_GROUPS
    gw = heads * SSD_HEADDIM
    n = SSD_STATE
    bb = d_inner // n
    cb = bb + SSD_GROUPS
    nc = l // SSD_CHUNK
    sps = seqs_per_step
    rows = sps * l
    has_init = s0 is not None
    emit_state = state_out is not None
    assert nseq % sps == 0 and nc % 2 == 0
    h2 = cum.shape[2]
    cum = cum.reshape(nseq // sps, sps * nc, h2, SSD_CHUNK)
    dtt = dtt.reshape(nseq // sps, sps * nc, h2, SSD_CHUNK)

    def rb(b):
        return row_block0 + b

    def lay(width, off):
        return pl.BlockSpec((None, 1, width), lambda b, g: (layer, 0, off + g))

    state_spec = pl.BlockSpec((sps, None, 2, heads, SSD_HEADDIM, n), lambda b, g: (b, layer, 0, g, 0, 0))
    in_specs = [
        pl.BlockSpec((rows, gw), lambda b, g: (rb(b), g)),
        pl.BlockSpec((rows, gw), lambda b, g: (rb(b), g)),
        pl.BlockSpec((rows, n), lambda b, g: (rb(b), bb + g)),
        pl.BlockSpec((rows, n), lambda b, g: (rb(b), cb + g)),
        pl.BlockSpec((None, sps * nc, h2, SSD_CHUNK), lambda b, g: (b, 0, 0, 0)),
        pl.BlockSpec((None, sps * nc, h2, SSD_CHUNK), lambda b, g: (b, 0, 0, 0)),
        lay(gw, 0),
        lay(gw, 0),
        pl.BlockSpec(rep.shape, lambda b, g: (0, 0)),
    ]
    args = [z, xbc, xbc, xbc, cum, dtt, dvec, norm_w, rep]
    if has_init:
        in_specs.append(state_spec)
        args.append(s0)
    aliases = {}
    out_shape = [jax.ShapeDtypeStruct((t, d_inner), BF16), jax.ShapeDtypeStruct((t, 128), F32)]
    out_specs = [pl.BlockSpec((rows, gw), lambda b, g: (rb(b), g)), pl.BlockSpec((rows, 128), lambda b, g: (rb(b), 0))]
    any_spec = pl.BlockSpec(memory_space=pl.ANY)

    def alias(buf, out_idx):
        aliases[len(args)] = out_idx
        in_specs.append(any_spec)
        args.append(buf)

    if bufs is not None:
        alias(bufs[0], 0)
        alias(bufs[1], 1)
    if emit_state:
        if not isinstance(state_out, jax.ShapeDtypeStruct):
            alias(state_out, 2)
        out_shape.append(jax.ShapeDtypeStruct(state_out.shape, F32))
        out_specs.append(state_spec)
    return pl.pallas_call(
        functools.partial(_ssd_main_body, l=l, seqs_per_step=sps, has_init=has_init, emit_state=emit_state,
                          n_aliased=len(aliases)),
        grid=(nseq // sps, SSD_GROUPS),
        in_specs=in_specs,
        out_specs=out_specs,
        out_shape=out_shape,
        input_output_aliases=aliases,
        scratch_shapes=[
            pltpu.VMEM((sps * nc, n, gw), F32),
            pltpu.VMEM((sps * nc, n, gw), BF16),
            pltpu.VMEM((2, 2, n, gw), F32),
            pltpu.VMEM((2, SSD_CHUNK, gw), F32),
        ],
        compiler_params=_params(("arbitrary", "arbitrary")),
        name="ssd_scan_gate",
    )(*args)


def _head_lane_spreader(heads, gw):
    hd2 = 2 * heads
    rows = jnp.arange(SSD_CHUNK)[:, None]
    lanes = jnp.arange(2 * gw)[None, :]
    hit = (rows >= hd2) & (rows < hd2 * (1 + N_SPLIT)) & ((rows % hd2) == lanes // SSD_HEADDIM)
    return hit.astype(BF16)


def _attn_body(*refs, l, tq, has_cache, rope, emit_kv, scale, n_aliased):
    it = iter(refs)
    q_ref, k_ref, v_ref, qn_ref, kn_ref = [next(it) for _ in range(5)]
    cos_ref, sin_ref = (next(it), next(it)) if rope else (None, None)
    ck_ref, cv_ref = (next(it), next(it)) if has_cache else (None, None)
    for _ in range(n_aliased):
        next(it)
    o_ref = next(it)
    ko_ref, vo_ref = (next(it), next(it)) if emit_kv else (None, None)
    kb_ref = next(it)

    lane = lax.broadcasted_iota(jnp.int32, (1, HEAD_DIM), 1)
    even = (lane % 2) == 0
    contract_last = (((1,), (1,)), ((), ()))

    def norm_rope(x, w, rows):
        r = lax.rsqrt(jnp.mean(x * x, axis=-1, keepdims=True) + EPS)
        xn = (x * r) * w
        if rope:
            partner = jnp.where(even, pltpu.roll(xn, HEAD_DIM - 1, 1), pltpu.roll(xn, 1, 1))
            xn = xn * cos_ref[rows, :] + partner * sin_ref[rows, :]
        return xn

    def kv_head(kv):
        kcols = slice(kv * HEAD_DIM, (kv + 1) * HEAD_DIM)
        for c in range(l // tq):
            rows = slice(c * tq, (c + 1) * tq)
            kn = norm_rope(k_ref[rows, kcols], kn_ref[...], rows)
            kb_ref[rows, kcols] = kn.astype(BF16)
            if emit_kv:
                ko_ref[rows, kcols] = kn
        v = v_ref[:, kcols]
        if emit_kv:
            vo_ref[:, kcols] = v
        vb = v.astype(BF16)
        kb = kb_ref[:, kcols]
        if has_cache:
            ckb = ck_ref[:, kcols].astype(BF16)
            cvb = cv_ref[:, kcols].astype(BF16)

        for c in range(l // tq):
            rows = slice(c * tq, (c + 1) * tq)
            for hh in range(GQA_GROUP):
                head = kv * GQA_GROUP + hh
                cols = slice(head * HEAD_DIM, (head + 1) * HEAD_DIM)
                qn = norm_rope(q_ref[rows, cols], qn_ref[...], rows).astype(BF16)
                s = lax.dot_general(qn, kb, contract_last, preferred_element_type=F32) * scale
                mx = jnp.max(s, axis=-1, keepdims=True)
                if has_cache:
                    sc = lax.dot_general(qn, ckb, contract_last, preferred_element_type=F32) * scale
                    mx = jnp.maximum(mx, jnp.max(sc, axis=-1, keepdims=True))
                p = jnp.exp(s - mx)
                den = jnp.sum(p, axis=-1, keepdims=True)
                o = jnp.dot(p.astype(BF16), vb, preferred_element_type=F32)
                if has_cache:
                    pc = jnp.exp(sc - mx)
                    den = den + jnp.sum(pc, axis=-1, keepdims=True)
                    o = o + jnp.dot(pc.astype(BF16), cvb, preferred_element_type=F32)
                o_ref[rows, cols] = (o / den).astype(o_ref.dtype)

    for kv in range(k_ref.shape[1] // HEAD_DIM):
        kv_head(kv)


def _attention(qkv, q_norm, k_norm, tables, caches, o_buf, kv_out, *, row_block0, nseq, l, layer, n_heads, n_kv,
               kv_per_step):
    t = qkv.shape[0]
    kw = kv_per_step * HEAD_DIM
    gq = GQA_GROUP * kw
    assert n_kv % kv_per_step == 0 and n_heads % kv_per_step == 0
    kcol = n_heads // kv_per_step
    vcol = (n_heads + n_kv) // kv_per_step
    tq = min(l, 256)

    def rb(b):
        return row_block0 + b

    norm_spec = pl.BlockSpec((None, 1, HEAD_DIM), lambda b, j: (layer, 0, 0))
    in_specs = [
        pl.BlockSpec((l, gq), lambda b, j: (rb(b), j)),
        pl.BlockSpec((l, kw), lambda b, j: (rb(b), kcol + j)),
        pl.BlockSpec((l, kw), lambda b, j: (rb(b), vcol + j)),
        norm_spec, norm_spec,
    ]
    args = [qkv, qkv, qkv, q_norm, k_norm]
    if tables is not None:
        in_specs += [pl.BlockSpec((l, HEAD_DIM), lambda b, j: (0, 0))] * 2
        args += list(tables)
    if caches is not None:
        past = caches[0].shape[2]
        in_specs += [pl.BlockSpec((None, None, past, kw), lambda b, j: (b, layer, 0, j))] * 2
        args += list(caches)
    any_spec = pl.BlockSpec(memory_space=pl.ANY)
    aliases = {}

    def alias(buf, out_idx):
        aliases[len(args)] = out_idx
        in_specs.append(any_spec)
        args.append(buf)

    if o_buf is not None:
        alias(o_buf, 0)
    out_shape = [jax.ShapeDtypeStruct((t, n_heads * HEAD_DIM), BF16)]
    out_specs = [pl.BlockSpec((l, gq), lambda b, j: (rb(b), j))]
    if kv_out is not None:
        fresh = isinstance(kv_out, jax.ShapeDtypeStruct)
        for k in range(2):
            if not fresh:
                alias(kv_out[k], 1 + k)
            out_shape.append(jax.ShapeDtypeStruct(kv_out.shape if fresh else kv_out[k].shape, F32))
            out_specs.append(pl.BlockSpec((None, None, l, kw), lambda b, j: (b, layer, 0, j)))
    return pl.pallas_call(
        functools.partial(_attn_body, l=l, tq=tq, has_cache=caches is not None, rope=tables is not None,
                          emit_kv=kv_out is not None, scale=HEAD_DIM ** -0.5, n_aliased=len(aliases)),
        grid=(nseq, n_kv // kv_per_step),
        in_specs=in_specs,
        out_specs=out_specs,
        out_shape=out_shape,
        input_output_aliases=aliases,
        scratch_shapes=[pltpu.VMEM((l, kw), BF16)],
        compiler_params=_params(("arbitrary", "arbitrary")),
        name="gqa_attention",
    )(*args)


def _rope_tables(n_tok):
    pairs = HEAD_DIM // 4
    rows = n_tok // GRID_W
    row_id = jnp.broadcast_to(jnp.arange(rows, dtype=F32)[:, None], (rows, GRID_W)).reshape(-1)
    col_id = jnp.broadcast_to(jnp.arange(GRID_W, dtype=F32)[None, :], (rows, GRID_W)).reshape(-1)
    inv = ROPE_THETA ** (-jnp.arange(pairs, dtype=F32) / pairs)
    ang = jnp.concatenate([row_id[:, None] * inv, col_id[:, None] * inv], axis=-1)
    cos, sin = jnp.cos(ang), jnp.sin(ang)
    cos_full = jnp.repeat(cos, 2, axis=-1)
    sin_signed = jnp.stack([-sin, sin], axis=-1).reshape(n_tok, HEAD_DIM)
    return cos_full, sin_signed


def kernel(x_prompt, x_sample, c, state_ssd, cache_k, cache_v, c_ctx, ada_w, ada_b, norm_pre, norm_post, ffn_w1, ffn_w3, ffn_w2, ssd_in_w, ssd_conv_w, ssd_conv_b, ssd_dt_bias, ssd_a_log, ssd_d, ssd_norm_w, ssd_out_w, attn_qkv_w, attn_q_norm, attn_k_norm, attn_o_w):
    batch, seq, d = x_prompt.shape
    dec_batch, dec_seq, _ = x_sample.shape
    depth = ada_w.shape[0]
    n_ctx = batch * seq
    n_lat = dec_batch * dec_seq
    t = n_ctx + n_lat
    d_inner = ssd_out_w.shape[1]
    n_ssd_heads = ssd_d.shape[1]
    n_kv = cache_k.shape[3]
    n_heads = attn_o_w.shape[1] // HEAD_DIM
    n_ssd_layers = ssd_in_w.shape[0]
    n_attn_layers = attn_qkv_w.shape[0]
    past = cache_k.shape[2]
    assert dec_seq % ROW_TILE == 0 and n_ctx % ROW_TILE == 0 and dec_batch + 1 <= N_COND_ROWS
    ctx_rb0, lat_rb0 = 0, n_ctx // dec_seq

    cond = jnp.concatenate([c_ctx[None, :], c, jnp.zeros((N_COND_ROWS - 1 - dec_batch, d), F32)], axis=0)
    modr = _modulation(cond, ada_w, ada_b).reshape(depth * N_COND_ROWS * N_MOD, 1, d)
    norm_pre_r = norm_pre.reshape(depth, 3, 1, d)
    norm_post_r = norm_post.reshape(depth, 3, 1, d)

    def postpre(xs, y, post, pre):
        post_a = pre_a = None
        if post is not None:
            l, j, res_w = post
            post_a = (norm_post_r[l, j], (l * N_COND_ROWS) * N_MOD + 3 * j, res_w)
        if pre is not None:
            l, j = pre
            pre_a = (norm_pre_r[l, j], (l * N_COND_ROWS) * N_MOD + 3 * j)
        outs = _postpre(xs[0], xs[1], y, modr, rows_per_cond_b=dec_seq, post=post_a, pre=pre_a)
        new_xs = (outs[0], outs[1]) if post is not None else xs
        return new_xs, (outs[-1] if pre is not None else None)

    def ffn(h, l, k):
        a = _gateup(h, ffn_w1, ffn_w3, (l, k), tm=ROW_TILE, tn=512)
        return _mm(a, ffn_w2, (l, k), tm=640, tn=512, out_dtype=F32)

    cos_sin = _rope_tables(dec_seq)
    cache_k2 = cache_k.reshape(dec_batch, n_attn_layers, past, n_kv * HEAD_DIM)
    cache_v2 = cache_v.reshape(dec_batch, n_attn_layers, past, n_kv * HEAD_DIM)
    q_norm_r = attn_q_norm.reshape(n_attn_layers, 1, HEAD_DIM)
    k_norm_r = attn_k_norm.reshape(n_attn_layers, 1, HEAD_DIM)
    new_kv = jax.ShapeDtypeStruct((batch, n_attn_layers, seq, n_kv * HEAD_DIM), F32)
    new_state = jax.ShapeDtypeStruct((batch, n_ssd_layers, 2, n_ssd_heads, SSD_HEADDIM, SSD_STATE), F32)
    conv_dim = ssd_conv_w.shape[-1]
    dt_width = 2 * n_ssd_heads
    assert seq % SSD_CHUNK == 0 and ROW_TILE % seq == 0 and dec_seq == ROW_TILE
    assert d_inner % 1024 == 0 and conv_dim % 1024 == 0 and (d_inner + conv_dim) % dt_width == 0
    conv_b_r = ssd_conv_b.reshape(n_ssd_layers, 1, conv_dim)
    dvec_r = jnp.repeat(ssd_d, SSD_HEADDIM, axis=-1).reshape(n_ssd_layers, 1, d_inner)
    ssd_nw_r = ssd_norm_w.reshape(n_ssd_layers, 1, d_inner)
    group_heads = n_ssd_heads // SSD_GROUPS
    rep = _head_lane_spreader(group_heads, group_heads * SSD_HEADDIM)
    n_mix = depth // n_attn_layers

    xs = (x_prompt.reshape(n_ctx, d), x_sample.reshape(n_lat, d))
    _, h = postpre(xs, None, None, (0, 0))
    for l in range(depth):
        y = ffn(h, l, 0)
        xs, h = postpre(xs, y, (l, 0, 0.5), (l, 1))
        i = l // n_mix
        if l % n_mix == 0:
            z = _mm(h, ssd_in_w, (i,), tm=ROW_TILE, tn=1024, out_dtype=F32, n_out=d_inner)
            xbc = _mm_conv(h, ssd_in_w, (i,), ssd_conv_w, conv_b_r, i, tm=ROW_TILE, tn=1024, col0=d_inner // 1024,
                           n_out=conv_dim, seq=seq, n_short_tiles=n_ctx // ROW_TILE)
            dt_raw = _mm(h, ssd_in_w, (i,), tm=ROW_TILE, tn=dt_width, out_dtype=F32,
                         col0=(d_inner + conv_dim) // dt_width, n_out=dt_width)
            cum, dtt = _ssd_prep(dt_raw, ssd_dt_bias[i], ssd_a_log[i], row_block0=ctx_rb0, nseq=batch, l=seq)
            u, ss, new_state = _ssd_main(z, xbc, cum, dtt, dvec_r, ssd_nw_r, rep, None, None, new_state,
                                         row_block0=ctx_rb0, nseq=batch, l=seq, seqs_per_step=ROW_TILE // seq,
                                         layer=i, d_inner=d_inner)
            cum, dtt = _ssd_prep(dt_raw, ssd_dt_bias[i], ssd_a_log[i], row_block0=lat_rb0, nseq=dec_batch, l=dec_seq)
            u, ss = _ssd_main(z, xbc, cum, dtt, dvec_r, ssd_nw_r, rep, state_ssd, (u, ss), None,
                              row_block0=lat_rb0, nseq=dec_batch, l=dec_seq, seqs_per_step=1, layer=i,
                              d_inner=d_inner)
            y = _mm(u, ssd_out_w, (i,), tm=ROW_TILE, tn=512, out_dtype=F32, ss=ss, inv_width=1.0 / d_inner)
        else:
            qkv = _mm(h, attn_qkv_w, (i,), tm=ROW_TILE, tn=1024, out_dtype=F32)
            o, *new_kv = _attention(qkv, q_norm_r, k_norm_r, None, None, None, new_kv,
                                    row_block0=ctx_rb0, nseq=batch, l=seq, layer=i, n_heads=n_heads, n_kv=n_kv,
                                    kv_per_step=n_kv)
            (o,) = _attention(qkv, q_norm_r, k_norm_r, cos_sin, (cache_k2, cache_v2), o, None,
                              row_block0=lat_rb0, nseq=dec_batch, l=dec_seq, layer=i, n_heads=n_heads, n_kv=n_kv,
                              kv_per_step=1)
            y = _mm(o, attn_o_w, (i,), tm=ROW_TILE, tn=1024, out_dtype=F32)
        xs, h = postpre(xs, y, (l, 1, 1.0), (l, 2))
        y = ffn(h, l, 1)
        xs, h = postpre(xs, y, (l, 2, 0.5), (l + 1, 0) if l + 1 < depth else None)

    y_prompt = xs[0].reshape(batch, seq, d)
    y_sample = xs[1].reshape(dec_batch, dec_seq, d)
    new_cache_k = new_kv[0].reshape(batch, n_attn_layers, seq, n_kv, HEAD_DIM)
    new_cache_v = new_kv[1].reshape(batch, n_attn_layers, seq, n_kv, HEAD_DIM)
    return (y_prompt, y_sample, new_state.astype(x_prompt.dtype), new_cache_k, new_cache_v)
```
